```python
import math
import jax, jax.numpy as jnp
from jax import lax
import numpy as np

D_MODEL = 1024
BATCH = 8
SEQ = 8192
DEPTH = 2

N_MIXERS = 2
N_ATTN_LAYERS = (DEPTH + 1) // 2
N_GLA_LAYERS = DEPTH // 2

A_HEADS = 16
A_KV_HEADS = 4
A_HEAD_DIM = D_MODEL // A_HEADS
A_GROUP = A_HEADS // A_KV_HEADS
WINDOW = 128
A_BLOCK = 128
ROPE_THETA = 10000.0
A_IN_COLS = (A_HEADS + 2 * A_KV_HEADS) * A_HEAD_DIM

B_HEADS = 4
B_KEY_DIM = (D_MODEL // 2) // B_HEADS
B_VAL_DIM = D_MODEL // B_HEADS
B_GATE_RANK = 16
B_GATE_TAU = 16.0
B_CHUNK = 64
B_QK_COLS = B_HEADS * B_KEY_DIM
B_V_COLS = B_HEADS * B_VAL_DIM
B_IN_COLS = 2 * B_QK_COLS + 2 * B_V_COLS + 2 * B_GATE_RANK

D_FF = 4 * D_MODEL
DN_ALPHA = float((2 * DEPTH) ** 0.25)
DN_BETA = float((8 * DEPTH) ** -0.25)
LN_EPS = 1e-5
HEAD_NORM_EPS = 1e-6

kernel_name = 'hybrid_swa_gla_deepnorm_encoder'


def _layer_norm(x, g, b):
    xf = x.astype(jnp.float32)
    mu = jnp.mean(xf, axis=-1, keepdims=True)
    var = jnp.mean(jnp.square(xf - mu), axis=-1, keepdims=True)
    y = (xf - mu) * lax.rsqrt(var + LN_EPS)
    return (y * g.astype(jnp.float32) + b.astype(jnp.float32)).astype(x.dtype)


def _rope_tables(positions):
    inv_freq = ROPE_THETA ** (-jnp.arange(0, A_HEAD_DIM, 2, dtype=jnp.float32) / A_HEAD_DIM)
    ang = positions.astype(jnp.float32)[..., None] * inv_freq
    return jnp.cos(ang)[:, :, None, :], jnp.sin(ang)[:, :, None, :]


def _rope(t, cos, sin):
    tf = t.astype(jnp.float32)
    t1, t2 = jnp.split(tf, 2, axis=-1)
    return jnp.concatenate([t1 * cos - t2 * sin, t2 * cos + t1 * sin], axis=-1).astype(t.dtype)


def _window_gqa(x, w_in, sink, w_out, cos, sin):
    B, S, _ = x.shape
    hd = A_HEAD_DIM
    h = x @ w_in
    q, k, v = jnp.split(h, [A_HEADS * hd, (A_HEADS + A_KV_HEADS) * hd], axis=-1)
    q = _rope(q.reshape(B, S, A_HEADS, hd), cos, sin)
    k = _rope(k.reshape(B, S, A_KV_HEADS, hd), cos, sin)
    v = v.reshape(B, S, A_KV_HEADS, hd)
    nblk = S // A_BLOCK
    span = A_BLOCK + 2 * WINDOW
    pad = ((0, 0), (WINDOW, WINDOW), (0, 0), (0, 0))
    kp = jnp.pad(k, pad)
    vp = jnp.pad(v, pad)
    qb = q.reshape(B, nblk, A_BLOCK, A_KV_HEADS, A_GROUP, hd).transpose(1, 0, 2, 3, 4, 5)
    rel = jnp.arange(A_BLOCK)[:, None] + WINDOW - jnp.arange(span)[None, :]
    band = jnp.abs(rel) <= WINDOW
    sink_l = sink.astype(jnp.float32).reshape(1, A_KV_HEADS, A_GROUP, 1, 1)
    scale = hd ** -0.5

    def block(args):
        qi, i = args
        start = i * A_BLOCK
        ks = lax.dynamic_slice_in_dim(kp, start, span, axis=1)
        vs = lax.dynamic_slice_in_dim(vp, start, span, axis=1)
        kpos = start - WINDOW + jnp.arange(span)
        valid = band & ((kpos >= 0) & (kpos < S))[None, :]
        s = jnp.einsum('bqkgd,bskd->bkgqs', qi, ks, preferred_element_type=jnp.float32) * scale
        s = jnp.where(valid, s, -jnp.inf)
        sink_col = jnp.broadcast_to(sink_l, s.shape[:-1] + (1,))
        p = jax.nn.softmax(jnp.concatenate([s, sink_col], axis=-1), axis=-1)[..., :span]
        return jnp.einsum('bkgqs,bskd->bqkgd', p.astype(vs.dtype), vs)

    o = lax.map(block, (qb, jnp.arange(nblk)))
    o = o.transpose(1, 0, 2, 3, 4, 5).reshape(B, S, A_HEADS * hd)
    return o @ w_out


def _gla_chunked(q, k, v, g):
    B, S, H, dk = q.shape
    dv = v.shape[-1]
    n = S // B_CHUNK

    def to_chunks(t):
        return t.reshape((B, n, B_CHUNK) + t.shape[2:]).swapaxes(0, 1)

    tril = jnp.tril(jnp.ones((B_CHUNK, B_CHUNK), dtype=bool))

    def step(state, inp):
        qc, kc, vc, gc = inp
        b = jnp.cumsum(gc, axis=1)
        qf = qc.astype(jnp.float32)
        kf = kc.astype(jnp.float32)
        vf = vc.astype(jnp.float32)
        qe = qf * jnp.exp(b)
        ke = kf * jnp.exp(-b)
        att = jnp.where(tril, jnp.einsum('bthd,bshd->bhts', qe, ke), 0.0)
        o = jnp.einsum('bhts,bshv->bthv', att, vf) + jnp.einsum('bthd,bhdv->bthv', qe, state)
        b_last = b[:, -1]
        kd = kf * jnp.exp(b_last[:, None] - b)
        state = jnp.exp(b_last)[..., None] * state + jnp.einsum('bshd,bshv->bhdv', kd, vf)
        return state, o.astype(vc.dtype)

    s0 = jnp.zeros((B, H, dk, dv), jnp.float32)
    _, o = lax.scan(step, s0, (to_chunks(q), to_chunks(k), to_chunks(v), to_chunks(g)))
    return o.swapaxes(0, 1).reshape(B, S, H, dv)


def _bidir_gla(x, w_in, gw2_f, gb_f, gw2_b, gb_b, norm_g, w_out):
    B, S, _ = x.shape
    h = x @ w_in
    c1 = B_QK_COLS
    c2 = 2 * B_QK_COLS
    c3 = c2 + B_V_COLS
    c4 = c3 + B_V_COLS
    c5 = c4 + B_GATE_RANK
    q, k, v, r, lr_f, lr_b = jnp.split(h, [c1, c2, c3, c4, c5], axis=-1)
    q = q.reshape(B, S, B_HEADS, B_KEY_DIM) * (B_KEY_DIM ** -0.5)
    k = k.reshape(B, S, B_HEADS, B_KEY_DIM)
    v = v.reshape(B, S, B_HEADS, B_VAL_DIM)

    def log_gate(lr, w2, bias):
        z = (lr @ w2 + bias).astype(jnp.float32)
        return (jax.nn.log_sigmoid(z) / B_GATE_TAU).reshape(B, S, B_HEADS, B_KEY_DIM)

    g_f = log_gate(lr_f, gw2_f, gb_f)
    g_b = log_gate(lr_b, gw2_b, gb_b)
    o_f = _gla_chunked(q, k, v, g_f)
    flip = lambda t: jnp.flip(t, axis=1)
    o_b = flip(_gla_chunked(flip(q), flip(k), flip(v), flip(g_b)))
    of = o_f.astype(jnp.float32) + o_b.astype(jnp.float32)
    of = of * lax.rsqrt(jnp.mean(jnp.square(of), axis=-1, keepdims=True) + HEAD_NORM_EPS)
    of = of * norm_g.astype(jnp.float32)
    o = of.astype(x.dtype).reshape(B, S, B_V_COLS) * jax.nn.silu(r)
    return o @ w_out


def _sqrelu_mlp(x, w1, w2):
    return jnp.square(jax.nn.relu(x @ w1)) @ w2


def setup_inputs(seed: int = 0) -> dict:
    key = jax.random.key(seed)
    ks = jax.random.split(key, 20)
    f32 = jnp.float32
    nrm = lambda kk, shape, s: jax.random.normal(kk, shape, f32) * s
    x = jax.random.normal(ks[0], (BATCH, SEQ, D_MODEL), f32)
    offs = jax.random.randint(ks[1], (BATCH, 1), 0, 4096, dtype=jnp.int32)
    positions = (jnp.arange(SEQ, dtype=jnp.int32)[None, :] + offs).astype(jnp.int32)
    d_inv = D_MODEL ** -0.5
    return {
        'x': x,
        'positions': positions,
        'attn_w_in': nrm(ks[2], (N_ATTN_LAYERS, D_MODEL, A_IN_COLS), d_inv),
        'attn_sink': nrm(ks[3], (N_ATTN_LAYERS, A_HEADS), 0.5),
        'attn_w_out': nrm(ks[4], (N_ATTN_LAYERS, A_HEADS * A_HEAD_DIM, D_MODEL), d_inv * DN_BETA),
        'gla_w_in': nrm(ks[5], (N_GLA_LAYERS, D_MODEL, B_IN_COLS), d_inv),
        'gla_gate_w2_fwd': nrm(ks[6], (N_GLA_LAYERS, B_GATE_RANK, B_QK_COLS), B_GATE_RANK ** -0.5),
        'gla_gate_b_fwd': nrm(ks[7], (N_GLA_LAYERS, B_QK_COLS), 0.02),
        'gla_gate_w2_bwd': nrm(ks[8], (N_GLA_LAYERS, B_GATE_RANK, B_QK_COLS), B_GATE_RANK ** -0.5),
        'gla_gate_b_bwd': nrm(ks[9], (N_GLA_LAYERS, B_QK_COLS), 0.02),
        'gla_norm_g': 1.0 + nrm(ks[10], (N_GLA_LAYERS, B_VAL_DIM), 0.02),
        'gla_w_out': nrm(ks[11], (N_GLA_LAYERS, B_V_COLS, D_MODEL), (B_V_COLS ** -0.5) * DN_BETA),
        'mix_ln_g': 1.0 + nrm(ks[12], (DEPTH, D_MODEL), 0.02),
        'mix_ln_b': nrm(ks[13], (DEPTH, D_MODEL), 0.02),
        'mlp_w1': nrm(ks[14], (DEPTH, D_MODEL, D_FF), d_inv),
        'mlp_w2': nrm(ks[15], (DEPTH, D_FF, D_MODEL), (D_FF ** -0.5) * DN_BETA),
        'mlp_ln_g': 1.0 + nrm(ks[16], (DEPTH, D_MODEL), 0.02),
        'mlp_ln_b': nrm(ks[17], (DEPTH, D_MODEL), 0.02),
    }


def reference(x, positions, attn_w_in, attn_sink, attn_w_out, gla_w_in, gla_gate_w2_fwd,
              gla_gate_b_fwd, gla_gate_w2_bwd, gla_gate_b_bwd, gla_norm_g, gla_w_out,
              mix_ln_g, mix_ln_b, mlp_w1, mlp_w2, mlp_ln_g, mlp_ln_b):
    cos, sin = _rope_tables(positions)
    for i in range(DEPTH):
        j = i // N_MIXERS
        if i % N_MIXERS == 0:
            y = _window_gqa(x, attn_w_in[j], attn_sink[j], attn_w_out[j], cos, sin)
        else:
            y = _bidir_gla(x, gla_w_in[j], gla_gate_w2_fwd[j], gla_gate_b_fwd[j],
                           gla_gate_w2_bwd[j], gla_gate_b_bwd[j], gla_norm_g[j], gla_w_out[j])
        x = _layer_norm(DN_ALPHA * x + y, mix_ln_g[i], mix_ln_b[i])
        x = _layer_norm(DN_ALPHA * x + _sqrelu_mlp(x, mlp_w1[i], mlp_w2[i]), mlp_ln_g[i], mlp_ln_b[i])
    return x
```

```python
import functools

import jax
import jax.numpy as jnp
from jax import lax
from jax.experimental import pallas as pl
from jax.experimental.pallas import tpu as pltpu

D_MODEL = 1024
DEPTH = 2

A_HEADS = 16
A_KV_HEADS = 4
A_HEAD_DIM = D_MODEL // A_HEADS
A_GROUP = A_HEADS // A_KV_HEADS
WINDOW = 128
ROPE_THETA = 10000.0

B_HEADS = 4
B_KEY_DIM = (D_MODEL // 2) // B_HEADS
B_VAL_DIM = D_MODEL // B_HEADS
B_GATE_RANK = 16
B_GATE_TAU = 16.0
B_CHUNK = 64
B_QK_COLS = B_HEADS * B_KEY_DIM
B_V_COLS = B_HEADS * B_VAL_DIM

D_FF = 4 * D_MODEL
DN_ALPHA = float((2 * DEPTH) ** 0.25)
LN_EPS = 1e-5
HEAD_NORM_EPS = 1e-6

LANES = 128
VMEM_LIMIT_BYTES = 56 * 1024 * 1024

TOKEN_TILE = 512
ATTN_Q_TILE = 512
GLA_TILE = 512
FF_CHUNK = 1024

BF16 = jnp.bfloat16
F32 = jnp.float32


def _const_spec(shape):
    zeros = (0,) * len(shape)
    return pl.BlockSpec(shape, lambda *_: zeros, pipeline_mode=pl.Buffered(1))


def _compiler_params(semantics):
    return pltpu.CompilerParams(dimension_semantics=semantics, vmem_limit_bytes=VMEM_LIMIT_BYTES)


def _layer_norm(z, g, b):
    mu = jnp.mean(z, axis=-1, keepdims=True)
    zc = z - mu
    var = jnp.mean(zc * zc, axis=-1, keepdims=True)
    return zc * lax.rsqrt(var + LN_EPS) * g + b


def _lane_index(shape):
    return lax.broadcasted_iota(jnp.int32, shape, len(shape) - 1)


def _attn_in_kernel(x_ref, ang_ref, w_ref, q_ref, k_ref, v_ref):
    h = jnp.dot(x_ref[...].astype(BF16), w_ref[...], preferred_element_type=F32)
    ang = ang_ref[...]
    lane = _lane_index(ang.shape)
    first_half = (lane % A_HEAD_DIM) < (A_HEAD_DIM // 2)
    cos = jnp.cos(ang)
    sin = jnp.sin(ang)
    sin_signed = jnp.where(first_half, -sin, sin)
    low_head = lane < A_HEAD_DIM

    def rope(t):
        partner = jnp.where(first_half, pltpu.roll(t, LANES - A_HEAD_DIM // 2, 1),
                            pltpu.roll(t, A_HEAD_DIM // 2, 1))
        return t * cos + partner * sin_signed

    def duplicate_heads(t):
        swapped = pltpu.roll(t, A_HEAD_DIM, 1)
        return jnp.where(low_head, t, swapped), jnp.where(low_head, swapped, t)

    scale = A_HEAD_DIM ** -0.5
    q_cols = A_HEADS * A_HEAD_DIM
    kv_cols = A_KV_HEADS * A_HEAD_DIM
    for j in range(q_cols // LANES):
        sl = slice(j * LANES, (j + 1) * LANES)
        q_ref[:, sl] = (rope(h[:, sl]) * scale).astype(BF16)
    for j in range(kv_cols // LANES):
        k_even, k_odd = duplicate_heads(rope(h[:, q_cols + j * LANES:q_cols + (j + 1) * LANES]))
        v_even, v_odd = duplicate_heads(h[:, q_cols + kv_cols + j * LANES:q_cols + kv_cols + (j + 1) * LANES])
        k_ref[:, (2 * j) * LANES:(2 * j + 1) * LANES] = k_even.astype(BF16)
        k_ref[:, (2 * j + 1) * LANES:(2 * j + 2) * LANES] = k_odd.astype(BF16)
        v_ref[:, (2 * j) * LANES:(2 * j + 1) * LANES] = v_even.astype(BF16)
        v_ref[:, (2 * j + 1) * LANES:(2 * j + 2) * LANES] = v_odd.astype(BF16)


def _attn_in_proj(x2, ang, w):
    n = x2.shape[0]
    tm = TOKEN_TILE
    kv_dup = A_KV_HEADS * LANES
    return pl.pallas_call(
        _attn_in_kernel,
        grid=(n // tm,),
        in_specs=[
            pl.BlockSpec((tm, D_MODEL), lambda i: (i, 0)),
            pl.BlockSpec((tm, LANES), lambda i: (i, 0)),
            _const_spec(w.shape),
        ],
        out_specs=[
            pl.BlockSpec((tm, A_HEADS * A_HEAD_DIM), lambda i: (i, 0)),
            pl.BlockSpec((tm, kv_dup), lambda i: (i, 0)),
            pl.BlockSpec((tm, kv_dup), lambda i: (i, 0)),
        ],
        out_shape=[
            jax.ShapeDtypeStruct((n, A_HEADS * A_HEAD_DIM), BF16),
            jax.ShapeDtypeStruct((n, kv_dup), BF16),
            jax.ShapeDtypeStruct((n, kv_dup), BF16),
        ],
        compiler_params=_compiler_params(("parallel",)),
        name="attn_in_proj",
    )(x2, ang, w)


def _window_attn_kernel(sink_ref, q_ref, kp_ref, km_ref, kn_ref, vp_ref, vm_ref, vn_ref, o_ref,
                        kbuf, vbuf, *, seq_len):
    tq = ATTN_Q_TILE
    blk = WINDOW
    span = 3 * blk
    i = pl.program_id(1)
    kbuf[0:blk, :] = kp_ref[...]
    kbuf[blk:blk + tq, :] = km_ref[...]
    kbuf[blk + tq:, :] = kn_ref[...]
    vbuf[0:blk, :] = vp_ref[...]
    vbuf[blk:blk + tq, :] = vm_ref[...]
    vbuf[blk + tq:, :] = vn_ref[...]

    row = lax.broadcasted_iota(jnp.int32, (blk, span), 0)
    col = lax.broadcasted_iota(jnp.int32, (blk, span), 1)
    band = jnp.abs(row + WINDOW - col) <= WINDOW
    lane = _lane_index((blk, LANES))
    low_head = lane < A_HEAD_DIM
    zero = jnp.zeros((blk, LANES), BF16)

    for jj in range(tq // blk):
        start = i * tq + jj * blk
        valid = band & (col >= WINDOW - start) & (col < seq_len + WINDOW - start)
        for g in range(A_KV_HEADS):
            parts = []
            for half in range(2):
                q2 = q_ref[jj * blk:(jj + 1) * blk, g * 2 * LANES + half * LANES:g * 2 * LANES + (half + 1) * LANES]
                parts.append(jnp.where(low_head, q2, zero))
                parts.append(jnp.where(low_head, zero, q2))
            lhs = jnp.concatenate(parts, axis=0)
            kk = kbuf[jj * blk:jj * blk + span, g * LANES:(g + 1) * LANES]
            vv = vbuf[jj * blk:jj * blk + span, g * LANES:(g + 1) * LANES]
            s = lax.dot_general(lhs, kk, (((1,), (1,)), ((), ())), preferred_element_type=F32)
            s = s.reshape(A_GROUP, blk, span)
            s = jnp.where(valid[None], s, -jnp.inf)
            sink = jnp.concatenate(
                [jnp.full((1, blk, 1), sink_ref[g * A_GROUP + hh], F32) for hh in range(A_GROUP)], axis=0)
            m = jnp.maximum(jnp.max(s, axis=-1, keepdims=True), sink)
            p = jnp.exp(s - m)
            denom = jnp.sum(p, axis=-1, keepdims=True) + jnp.exp(sink - m)
            o = jnp.dot(p.astype(BF16).reshape(A_GROUP * blk, span), vv, preferred_element_type=F32)
            o = o.reshape(A_GROUP, blk, LANES) * (1.0 / denom)
            for half in range(2):
                pair = jnp.where(low_head, o[2 * half], o[2 * half + 1])
                o_ref[jj * blk:(jj + 1) * blk,
                      g * 2 * LANES + half * LANES:g * 2 * LANES + (half + 1) * LANES] = pair.astype(BF16)


def _window_attention(q, kd, vd, sink, batch, seq_len):
    n = q.shape[0]
    tq = ATTN_Q_TILE
    blk = WINDOW
    nt = seq_len // tq
    per_tile = tq // blk
    nblk = seq_len // blk
    kv_dup = kd.shape[1]

    def main_map(b, i):
        return (b * nt + i, 0)

    def prev_map(b, i):
        return (b * nblk + jnp.maximum(i * per_tile - 1, 0), 0)

    def next_map(b, i):
        return (b * nblk + jnp.minimum((i + 1) * per_tile, nblk - 1), 0)

    kv_specs = [pl.BlockSpec((blk, kv_dup), prev_map), pl.BlockSpec((tq, kv_dup), main_map),
                pl.BlockSpec((blk, kv_dup), next_map)]
    return pl.pallas_call(
        functools.partial(_window_attn_kernel, seq_len=seq_len),
        grid=(batch, nt),
        in_specs=[pl.BlockSpec(memory_space=pltpu.SMEM), pl.BlockSpec((tq, q.shape[1]), main_map)]
        + kv_specs + kv_specs,
        out_specs=pl.BlockSpec((tq, q.shape[1]), main_map),
        out_shape=jax.ShapeDtypeStruct((n, q.shape[1]), BF16),
        scratch_shapes=[pltpu.VMEM((tq + 2 * blk, kv_dup), BF16), pltpu.VMEM((tq + 2 * blk, kv_dup), BF16)],
        compiler_params=_compiler_params(("parallel", "parallel")),
        name="window_attn",
    )(sink, q, kd, kd, kd, vd, vd, vd)


def _residual_mlp_tail(y, x, ln1_g, ln1_b, w1_ref, w2_ref, ln2_g, ln2_b):
    x1 = _layer_norm(DN_ALPHA * x + y, ln1_g, ln1_b)
    x1b = x1.astype(BF16)
    acc = None
    for c in range(D_FF // FF_CHUNK):
        hc = jnp.dot(x1b, w1_ref[:, c * FF_CHUNK:(c + 1) * FF_CHUNK], preferred_element_type=F32)
        hc = jnp.square(jnp.maximum(hc, 0.0)).astype(BF16)
        part = jnp.dot(hc, w2_ref[c * FF_CHUNK:(c + 1) * FF_CHUNK, :], preferred_element_type=F32)
        acc = part if acc is None else acc + part
    return _layer_norm(DN_ALPHA * x1 + acc, ln2_g, ln2_b)


def _attn_post_kernel(o_ref, x_ref, wo_ref, ln1g_ref, ln1b_ref, w1_ref, w2_ref, ln2g_ref, ln2b_ref, out_ref):
    y = jnp.dot(o_ref[...], wo_ref[...], preferred_element_type=F32)
    out_ref[...] = _residual_mlp_tail(y, x_ref[...], ln1g_ref[...], ln1b_ref[...], w1_ref, w2_ref,
                                      ln2g_ref[...], ln2b_ref[...])


def _gla_post_kernel(of_ref, ob_ref, r_ref, ng_ref, x_ref, wo_ref, ln1g_ref, ln1b_ref, w1_ref, w2_ref,
                     ln2g_ref, ln2b_ref, out_ref):
    o = of_ref[...] + ob_ref[...]
    r = r_ref[...]
    gate = r * (1.0 / (1.0 + jnp.exp(-r)))
    ng = ng_ref[...]
    heads = []
    for h in range(B_HEADS):
        oh = o[:, h * B_VAL_DIM:(h + 1) * B_VAL_DIM]
        oh = oh * lax.rsqrt(jnp.mean(oh * oh, axis=-1, keepdims=True) + HEAD_NORM_EPS)
        heads.append((oh * ng * gate[:, h * B_VAL_DIM:(h + 1) * B_VAL_DIM]).astype(BF16))
    y = jnp.dot(jnp.concatenate(heads, axis=-1), wo_ref[...], preferred_element_type=F32)
    out_ref[...] = _residual_mlp_tail(y, x_ref[...], ln1g_ref[...], ln1b_ref[...], w1_ref, w2_ref,
                                      ln2g_ref[...], ln2b_ref[...])


def _post_mixer(kernel_fn, mixer_inputs, extra_consts, x2, wo, ln1g, ln1b, w1, w2, ln2g, ln2b):
    n = x2.shape[0]
    tm = TOKEN_TILE
    tile = lambda cols: pl.BlockSpec((tm, cols), lambda i: (i, 0))
    consts = list(extra_consts) + []
    return pl.pallas_call(
        kernel_fn,
        grid=(n // tm,),
        in_specs=[tile(a.shape[1]) for a in mixer_inputs]
        + [_const_spec(c.shape) for c in consts]
        + [tile(D_MODEL), _const_spec(wo.shape), _const_spec(ln1g.shape), _const_spec(ln1b.shape),
           _const_spec(w1.shape), _const_spec(w2.shape), _const_spec(ln2g.shape), _const_spec(ln2b.shape)],
        out_specs=tile(D_MODEL),
        out_shape=jax.ShapeDtypeStruct((n, D_MODEL), F32),
        compiler_params=_compiler_params(("parallel",)),
        name=kernel_fn.__name__.strip("_"),
    )(*mixer_inputs, *consts, x2, wo, ln1g, ln1b, w1, w2, ln2g, ln2b)


def _gla_in_kernel(x_ref, w_ref, q_ref, k_ref, v_ref, r_ref, lr_ref):
    h = jnp.dot(x_ref[...].astype(BF16), w_ref[...], preferred_element_type=F32)
    c1 = B_QK_COLS
    c2 = 2 * B_QK_COLS
    c3 = c2 + B_V_COLS
    c4 = c3 + B_V_COLS
    q_ref[...] = h[:, :c1] * (B_KEY_DIM ** -0.5)
    k_ref[...] = h[:, c1:c2]
    v_ref[...] = h[:, c2:c3].astype(BF16)
    r_ref[...] = h[:, c3:c4]
    lr_ref[...] = h[:, c4:].astype(BF16)


def _gla_in_proj(x2, w):
    n = x2.shape[0]
    tm = TOKEN_TILE
    tile = lambda cols: pl.BlockSpec((tm, cols), lambda i: (i, 0))
    return pl.pallas_call(
        _gla_in_kernel,
        grid=(n // tm,),
        in_specs=[tile(D_MODEL), _const_spec(w.shape)],
        out_specs=[tile(B_QK_COLS), tile(B_QK_COLS), tile(B_V_COLS), tile(B_V_COLS), tile(LANES)],
        out_shape=[
            jax.ShapeDtypeStruct((n, B_QK_COLS), F32),
            jax.ShapeDtypeStruct((n, B_QK_COLS), F32),
            jax.ShapeDtypeStruct((n, B_V_COLS), BF16),
            jax.ShapeDtypeStruct((n, B_V_COLS), F32),
            jax.ShapeDtypeStruct((n, LANES), BF16),
        ],
        compiler_params=_compiler_params(("parallel",)),
        name="gla_in_proj",
    )(x2, w)


def _log_gate(lr, w2, bias):
    z = jnp.dot(lr, w2, preferred_element_type=F32) + bias
    softplus_neg = jnp.maximum(-z, 0.0) + jnp.log1p(jnp.exp(-jnp.abs(z)))
    return -softplus_neg / B_GATE_TAU


def _chunk_cumsum(g, reverse):
    rows = g.shape[0]
    pos = lax.broadcasted_iota(jnp.int32, g.shape, 0) % B_CHUNK
    shift = 1
    while shift < B_CHUNK:
        if reverse:
            g = g + jnp.where(pos < B_CHUNK - shift, pltpu.roll(g, rows - shift, 0), 0.0)
        else:
            g = g + jnp.where(pos >= shift, pltpu.roll(g, shift, 0), 0.0)
        shift *= 2
    return g


def _gla_direction_chunk(c, reverse, q_ref, k_ref, v_ref, b_ref, o_ref, state_ref):
    rows = pl.ds(pl.multiple_of(c * B_CHUNK, B_CHUNK), B_CHUNK)
    b = b_ref[rows, :]
    edge = 0 if reverse else B_CHUNK - 1
    b_edge = b[edge:edge + 1, :]
    q = q_ref[rows, :]
    k = k_ref[rows, :]
    qe = (q * jnp.exp(b)).astype(BF16)
    ke = (k * jnp.exp(-b)).astype(BF16)
    kd = (k * jnp.exp(b_edge - b)).astype(BF16)
    decay = jnp.exp(b_edge)
    t_row = lax.broadcasted_iota(jnp.int32, (B_CHUNK, B_CHUNK), 0)
    t_col = lax.broadcasted_iota(jnp.int32, (B_CHUNK, B_CHUNK), 1)
    causal = (t_row <= t_col) if reverse else (t_row >= t_col)
    for h in range(B_HEADS):
        ks = slice(h * B_KEY_DIM, (h + 1) * B_KEY_DIM)
        vs = slice(h * B_VAL_DIM, (h + 1) * B_VAL_DIM)
        vh = v_ref[rows, vs]
        att = lax.dot_general(qe[:, ks], ke[:, ks], (((1,), (1,)), ((), ())), preferred_element_type=F32)
        att = jnp.where(causal, att, 0.0).astype(BF16)
        state = state_ref[h]
        o = jnp.dot(att, vh, preferred_element_type=F32)
        o = o + lax.dot_general(qe[:, ks], state.astype(BF16), (((1,), (1,)), ((), ())),
                                preferred_element_type=F32)
        update = lax.dot_general(vh, kd[:, ks], (((0,), (0,)), ((), ())), preferred_element_type=F32)
        state_ref[h] = state * decay[:, ks] + update
        o_ref[rows, vs] = o


def _gla_scan_kernel(qf_ref, kf_ref, vf_ref, lrf_ref, qb_ref, kb_ref, vb_ref, lrb_ref,
                     w2f_ref, gbf_ref, w2b_ref, gbb_ref, of_ref, ob_ref,
                     bf_ref, bb_ref, sf_ref, sb_ref):
    @pl.when(pl.program_id(1) == 0)
    def _():
        sf_ref[...] = jnp.zeros_like(sf_ref)
        sb_ref[...] = jnp.zeros_like(sb_ref)

    bf_ref[...] = _chunk_cumsum(_log_gate(lrf_ref[...], w2f_ref[...], gbf_ref[...]), reverse=False)
    bb_ref[...] = _chunk_cumsum(_log_gate(lrb_ref[...], w2b_ref[...], gbb_ref[...]), reverse=True)
    n_chunks = GLA_TILE // B_CHUNK

    def body(c, carry):
        _gla_direction_chunk(c, False, qf_ref, kf_ref, vf_ref, bf_ref, of_ref, sf_ref)
        _gla_direction_chunk(n_chunks - 1 - c, True, qb_ref, kb_ref, vb_ref, bb_ref, ob_ref, sb_ref)
        return carry

    lax.fori_loop(0, n_chunks, body, 0)


def _gla_scan(q, k, v, lr, w2f, gbf, w2b, gbb, batch, seq_len):
    n = q.shape[0]
    ts = GLA_TILE
    nt = seq_len // ts

    def fwd_map(b, i):
        return (b * nt + i, 0)

    def bwd_map(b, i):
        return (b * nt + nt - 1 - i, 0)

    def token_specs(index_map):
        return [pl.BlockSpec((ts, B_QK_COLS), index_map), pl.BlockSpec((ts, B_QK_COLS), index_map),
                pl.BlockSpec((ts, B_V_COLS), index_map), pl.BlockSpec((ts, LANES), index_map)]

    return pl.pallas_call(
        _gla_scan_kernel,
        grid=(batch, nt),
        in_specs=token_specs(fwd_map) + token_specs(bwd_map)
        + [_const_spec(w2f.shape), _const_spec(gbf.shape), _const_spec(w2b.shape), _const_spec(gbb.shape)],
        out_specs=[pl.BlockSpec((ts, B_V_COLS), fwd_map), pl.BlockSpec((ts, B_V_COLS), bwd_map)],
        out_shape=[jax.ShapeDtypeStruct((n, B_V_COLS), F32), jax.ShapeDtypeStruct((n, B_V_COLS), F32)],
        scratch_shapes=[
            pltpu.VMEM((ts, B_QK_COLS), F32),
            pltpu.VMEM((ts, B_QK_COLS), F32),
            pltpu.VMEM((B_HEADS, B_VAL_DIM, B_KEY_DIM), F32),
            pltpu.VMEM((B_HEADS, B_VAL_DIM, B_KEY_DIM), F32),
        ],
        compiler_params=_compiler_params(("parallel", "arbitrary")),
        name="gla_scan",
    )(q, k, v, lr, q, k, v, lr, w2f, gbf, w2b, gbb)


def _row(v):
    return v.reshape(1, -1).astype(F32)


def _pad_gate_weight(w2, first_row):
    out = jnp.zeros((LANES, w2.shape[1]), BF16)
    return lax.dynamic_update_slice(out, w2.astype(BF16), (first_row, 0))


def kernel(x, positions, attn_w_in, attn_sink, attn_w_out, gla_w_in, gla_gate_w2_fwd, gla_gate_b_fwd,
           gla_gate_w2_bwd, gla_gate_b_bwd, gla_norm_g, gla_w_out, mix_ln_g, mix_ln_b, mlp_w1, mlp_w2,
           mlp_ln_g, mlp_ln_b):
    batch, seq_len, d_model = x.shape
    n = batch * seq_len
    x2 = x.reshape(n, d_model)

    inv_freq = ROPE_THETA ** (-jnp.arange(0, A_HEAD_DIM, 2, dtype=F32) / A_HEAD_DIM)
    ang = positions.astype(F32).reshape(n, 1) * jnp.tile(inv_freq, LANES // inv_freq.shape[0])[None, :]

    for i in range(DEPTH):
        j = i // 2
        w1 = mlp_w1[i].astype(BF16)
        w2 = mlp_w2[i].astype(BF16)
        tail = (_row(mix_ln_g[i]), _row(mix_ln_b[i]), w1, w2, _row(mlp_ln_g[i]), _row(mlp_ln_b[i]))
        if i % 2 == 0:
            q, kd, vd = _attn_in_proj(x2, ang, attn_w_in[j].astype(BF16))
            o = _window_attention(q, kd, vd, attn_sink[j].astype(F32), batch, seq_len)
            x2 = _post_mixer(_attn_post_kernel, [o], [], x2, attn_w_out[j].astype(BF16), *tail)
        else:
            w_in = gla_w_in[j]
            lr_cols = 2 * B_GATE_RANK
            w_in = jnp.pad(w_in, ((0, 0), (0, LANES - lr_cols))).astype(BF16)
            q, k, v, r, lr = _gla_in_proj(x2, w_in)
            o_f, o_b = _gla_scan(q, k, v, lr,
                                 _pad_gate_weight(gla_gate_w2_fwd[j], 0), _row(gla_gate_b_fwd[j]),
                                 _pad_gate_weight(gla_gate_w2_bwd[j], B_GATE_RANK), _row(gla_gate_b_bwd[j]),
                                 batch, seq_len)
            x2 = _post_mixer(_gla_post_kernel, [o_f, o_b, r], [_row(gla_norm_g[j])], x2,
                             gla_w_out[j].astype(BF16), *tail)
    return x2.reshape(batch, seq_len, d_model)
```

```python
import functools

import jax
import jax.numpy as jnp
from jax import lax
from jax.experimental import pallas as pl
from jax.experimental.pallas import tpu as pltpu

D_MODEL = 1024
DEPTH = 2

A_HEADS = 16
A_KV_HEADS = 4
A_HEAD_DIM = D_MODEL // A_HEADS
A_GROUP = A_HEADS // A_KV_HEADS
WINDOW = 128
ROPE_THETA = 10000.0

B_HEADS = 4
B_KEY_DIM = (D_MODEL // 2) // B_HEADS
B_VAL_DIM = D_MODEL // B_HEADS
B_GATE_RANK = 16
B_GATE_TAU = 16.0
B_CHUNK = 64
B_QK_COLS = B_HEADS * B_KEY_DIM
B_V_COLS = B_HEADS * B_VAL_DIM

D_FF = 4 * D_MODEL
DN_ALPHA = float((2 * DEPTH) ** 0.25)
LN_EPS = 1e-5
HEAD_NORM_EPS = 1e-6

LANES = 128
VMEM_LIMIT_BYTES = 56 * 1024 * 1024

TOKEN_TILE = 512
ATTN_Q_TILE = 512
GLA_TILE = 512
FF_CHUNK = 1024

BF16 = jnp.bfloat16
F32 = jnp.float32


def _const_spec(shape):
    zeros = (0,) * len(shape)
    return pl.BlockSpec(shape, lambda *_: zeros, pipeline_mode=pl.Buffered(1))


def _compiler_params(semantics, flags=None):
    return pltpu.CompilerParams(dimension_semantics=semantics, vmem_limit_bytes=VMEM_LIMIT_BYTES, flags=flags)


def _layer_norm(z, g, b):
    mu = jnp.mean(z, axis=-1, keepdims=True)
    zc = z - mu
    var = jnp.mean(zc * zc, axis=-1, keepdims=True)
    return zc * lax.rsqrt(var + LN_EPS) * g + b


def _lane_index(shape):
    return lax.broadcasted_iota(jnp.int32, shape, len(shape) - 1)


def _attn_in_kernel(x_ref, ang_ref, wqv_ref, wk_ref, qt_ref, k_ref, vt_ref):
    xb = x_ref[...].astype(BF16)
    half = A_HEAD_DIM // 2
    q_cols = A_HEADS * A_HEAD_DIM
    kv_cols = A_KV_HEADS * A_HEAD_DIM
    cos_t = jnp.cos(ang_ref[...])
    sin_t = jnp.sin(ang_ref[...])

    hqv = lax.dot_general(wqv_ref[...], xb, (((1,), (1,)), ((), ())), preferred_element_type=F32)
    scale = A_HEAD_DIM ** -0.5
    for h in range(A_HEADS):
        t1 = hqv[h * A_HEAD_DIM:h * A_HEAD_DIM + half, :]
        t2 = hqv[h * A_HEAD_DIM + half:(h + 1) * A_HEAD_DIM, :]
        qt_ref[h * A_HEAD_DIM:h * A_HEAD_DIM + half, :] = ((t1 * cos_t - t2 * sin_t) * scale).astype(BF16)
        qt_ref[h * A_HEAD_DIM + half:(h + 1) * A_HEAD_DIM, :] = ((t2 * cos_t + t1 * sin_t) * scale).astype(BF16)
    vt_ref[...] = hqv[q_cols:, :].astype(BF16)

    cos = jnp.concatenate([cos_t] * (LANES // half), axis=0).T
    sin = jnp.concatenate([sin_t] * (LANES // half), axis=0).T
    lane = _lane_index(cos.shape)
    first_half = (lane % A_HEAD_DIM) < half
    sin_signed = jnp.where(first_half, -sin, sin)
    low_head = lane < A_HEAD_DIM
    hk = jnp.dot(xb, wk_ref[...], preferred_element_type=F32)
    for j in range(kv_cols // LANES):
        t = hk[:, j * LANES:(j + 1) * LANES]
        partner = jnp.where(first_half, pltpu.roll(t, LANES - half, 1), pltpu.roll(t, half, 1))
        t = t * cos + partner * sin_signed
        swapped = pltpu.roll(t, A_HEAD_DIM, 1)
        k_ref[:, (2 * j) * LANES:(2 * j + 1) * LANES] = jnp.where(low_head, t, swapped).astype(BF16)
        k_ref[:, (2 * j + 1) * LANES:(2 * j + 2) * LANES] = jnp.where(low_head, swapped, t).astype(BF16)


def _attn_in_proj(x2, ang_t, wqv_t, wk):
    n = x2.shape[0]
    tm = TOKEN_TILE
    q_cols = A_HEADS * A_HEAD_DIM
    kv_cols = A_KV_HEADS * A_HEAD_DIM
    k_dup = A_KV_HEADS * LANES
    return pl.pallas_call(
        _attn_in_kernel,
        grid=(n // tm,),
        in_specs=[
            pl.BlockSpec((tm, D_MODEL), lambda i: (i, 0)),
            pl.BlockSpec((ang_t.shape[0], tm), lambda i: (0, i)),
            _const_spec(wqv_t.shape),
            _const_spec(wk.shape),
        ],
        out_specs=[
            pl.BlockSpec((q_cols, tm), lambda i: (0, i)),
            pl.BlockSpec((tm, k_dup), lambda i: (i, 0)),
            pl.BlockSpec((kv_cols, tm), lambda i: (0, i)),
        ],
        out_shape=[
            jax.ShapeDtypeStruct((q_cols, n), BF16),
            jax.ShapeDtypeStruct((n, k_dup), BF16),
            jax.ShapeDtypeStruct((kv_cols, n), BF16),
        ],
        compiler_params=_compiler_params(("parallel",)),
        name="attn_in_proj",
    )(x2, ang_t, wqv_t, wk)


def _window_attn_kernel(sink_ref, qt_ref, kp_ref, km_ref, kn_ref, vp_ref, vm_ref, vn_ref, o_ref,
                        kbuf, vbuf):
    tq = ATTN_Q_TILE
    blk = WINDOW
    span = 3 * blk
    i = pl.program_id(1)
    last_tile = pl.num_programs(1) - 1
    kbuf[0:blk, :] = kp_ref[...]
    kbuf[blk:blk + tq, :] = km_ref[...]
    kbuf[blk + tq:, :] = kn_ref[...]
    vbuf[:, 0:blk] = vp_ref[...]
    vbuf[:, blk:blk + tq] = vm_ref[...]
    vbuf[:, blk + tq:] = vn_ref[...]

    pair_shape = (blk, 2 * LANES)
    key = lax.broadcasted_iota(jnp.int32, pair_shape, 0)
    query = _lane_index(pair_shape) % blk
    neg_inf = jnp.float32(-jnp.inf)
    prev_bias = jnp.where(key >= query, 0.0, neg_inf)
    next_bias = jnp.where(key <= query, 0.0, neg_inf)
    first_prev_bias = jnp.where(i == 0, neg_inf, prev_bias)
    last_next_bias = jnp.where(i == last_tile, neg_inf, next_bias)
    low_rows = lax.broadcasted_iota(jnp.int32, (LANES, blk), 0) < A_HEAD_DIM
    zero = jnp.zeros((LANES, blk), BF16)

    n_blocks = tq // blk
    units = [(jj, g, pr) for jj in range(n_blocks) for g in range(A_KV_HEADS) for pr in range(A_GROUP // 2)]

    def pair_lanes(g, pr):
        return slice((g * 2 + pr) * LANES, (g * 2 + pr + 1) * LANES)

    def scores(jj, g, pr):
        kk = kbuf[jj * blk:jj * blk + span, g * LANES:(g + 1) * LANES]
        qt = qt_ref[pair_lanes(g, pr), jj * blk:(jj + 1) * blk]
        w = jnp.concatenate([jnp.where(low_rows, qt, zero), jnp.where(low_rows, zero, qt)], axis=1)
        return jnp.dot(kk, w, preferred_element_type=F32)

    def softmax(s, jj, g, pr):
        s_prev = s[0:blk] + (first_prev_bias if jj == 0 else prev_bias)
        s_mid = s[blk:2 * blk]
        s_next = s[2 * blk:] + (last_next_bias if jj == n_blocks - 1 else next_bias)
        head = g * A_GROUP + 2 * pr
        sink = jnp.concatenate([jnp.full((1, blk), sink_ref[head], F32),
                                jnp.full((1, blk), sink_ref[head + 1], F32)], axis=1)
        m = jnp.maximum(jnp.maximum(jnp.max(s_prev, axis=0, keepdims=True),
                                    jnp.max(s_mid, axis=0, keepdims=True)),
                        jnp.maximum(jnp.max(s_next, axis=0, keepdims=True), sink))
        p = jnp.concatenate([jnp.exp(s_prev - m), jnp.exp(s_mid - m), jnp.exp(s_next - m)], axis=0)
        denom = jnp.sum(p, axis=0, keepdims=True) + jnp.exp(sink - m)
        return p.astype(BF16), 1.0 / denom

    def weighted_values(p, inv_denom, jj, g, pr):
        vt = vbuf[g * A_HEAD_DIM:(g + 1) * A_HEAD_DIM, jj * blk:jj * blk + span]
        ot = jnp.dot(vt, p, preferred_element_type=F32) * inv_denom
        both = jnp.concatenate([ot[:, 0:blk], ot[:, blk:]], axis=0)
        o_ref[jj * blk:(jj + 1) * blk, pair_lanes(g, pr)] = both.T.astype(BF16)

    s_stage = {}
    p_stage = {}
    for n in range(len(units) + 2):
        if n < len(units):
            s_stage[n] = scores(*units[n])
        if 0 <= n - 1 < len(units):
            p_stage[n - 1] = softmax(s_stage.pop(n - 1), *units[n - 1])
        if 0 <= n - 2 < len(units):
            weighted_values(*p_stage.pop(n - 2), *units[n - 2])


def _window_attention(qt, kd, vt, sink, batch, seq_len):
    q_cols, n = qt.shape
    v_cols = vt.shape[0]
    tq = ATTN_Q_TILE
    blk = WINDOW
    nt = seq_len // tq
    per_tile = tq // blk
    nblk = seq_len // blk
    k_dup = kd.shape[1]

    def main_idx(b, i):
        return b * nt + i

    def prev_idx(b, i):
        return b * nblk + jnp.maximum(i * per_tile - 1, 0)

    def next_idx(b, i):
        return b * nblk + jnp.minimum((i + 1) * per_tile, nblk - 1)

    k_specs = [pl.BlockSpec((blk, k_dup), lambda b, i: (prev_idx(b, i), 0)),
               pl.BlockSpec((tq, k_dup), lambda b, i: (main_idx(b, i), 0)),
               pl.BlockSpec((blk, k_dup), lambda b, i: (next_idx(b, i), 0))]
    v_specs = [pl.BlockSpec((v_cols, blk), lambda b, i: (0, prev_idx(b, i))),
               pl.BlockSpec((v_cols, tq), lambda b, i: (0, main_idx(b, i))),
               pl.BlockSpec((v_cols, blk), lambda b, i: (0, next_idx(b, i)))]
    return pl.pallas_call(
        _window_attn_kernel,
        grid=(batch, nt),
        in_specs=[pl.BlockSpec(memory_space=pltpu.SMEM),
                  pl.BlockSpec((q_cols, tq), lambda b, i: (0, main_idx(b, i)))] + k_specs + v_specs,
        out_specs=pl.BlockSpec((tq, q_cols), lambda b, i: (main_idx(b, i), 0)),
        out_shape=jax.ShapeDtypeStruct((n, q_cols), BF16),
        scratch_shapes=[pltpu.VMEM((tq + 2 * blk, k_dup), BF16), pltpu.VMEM((v_cols, tq + 2 * blk), BF16)],
        compiler_params=_compiler_params(("parallel", "parallel")),
        name="window_attn",
    )(sink, qt, kd, kd, kd, vt, vt, vt)


def _residual_mlp_tail(y, x, ln1_g, ln1_b, w1_ref, w2_ref, ln2_g, ln2_b):
    x1 = _layer_norm(DN_ALPHA * x + y, ln1_g, ln1_b)
    x1b = x1.astype(BF16)
    acc = None
    for c in range(D_FF // FF_CHUNK):
        hc = jnp.dot(x1b, w1_ref[:, c * FF_CHUNK:(c + 1) * FF_CHUNK], preferred_element_type=F32)
        hc = jnp.square(jnp.maximum(hc, 0.0)).astype(BF16)
        part = jnp.dot(hc, w2_ref[c * FF_CHUNK:(c + 1) * FF_CHUNK, :], preferred_element_type=F32)
        acc = part if acc is None else acc + part
    return _layer_norm(DN_ALPHA * x1 + acc, ln2_g, ln2_b)


def _attn_post_kernel(o_ref, x_ref, wo_ref, ln1g_ref, ln1b_ref, w1_ref, w2_ref, ln2g_ref, ln2b_ref, out_ref):
    y = jnp.dot(o_ref[...], wo_ref[...], preferred_element_type=F32)
    out_ref[...] = _residual_mlp_tail(y, x_ref[...], ln1g_ref[...], ln1b_ref[...], w1_ref, w2_ref,
                                      ln2g_ref[...], ln2b_ref[...])


def _gla_post_kernel(of_ref, ob_ref, r_ref, ng_ref, x_ref, wo_ref, ln1g_ref, ln1b_ref, w1_ref, w2_ref,
                     ln2g_ref, ln2b_ref, out_ref):
    o = of_ref[...] + ob_ref[...]
    r = r_ref[...]
    gate = r * (1.0 / (1.0 + jnp.exp(-r)))
    ng = ng_ref[...]
    heads = []
    for h in range(B_HEADS):
        oh = o[:, h * B_VAL_DIM:(h + 1) * B_VAL_DIM]
        oh = oh * lax.rsqrt(jnp.mean(oh * oh, axis=-1, keepdims=True) + HEAD_NORM_EPS)
        heads.append((oh * ng * gate[:, h * B_VAL_DIM:(h + 1) * B_VAL_DIM]).astype(BF16))
    y = jnp.dot(jnp.concatenate(heads, axis=-1), wo_ref[...], preferred_element_type=F32)
    out_ref[...] = _residual_mlp_tail(y, x_ref[...], ln1g_ref[...], ln1b_ref[...], w1_ref, w2_ref,
                                      ln2g_ref[...], ln2b_ref[...])


def _post_mixer(kernel_fn, mixer_inputs, extra_consts, x2, wo, ln1g, ln1b, w1, w2, ln2g, ln2b):
    n = x2.shape[0]
    tm = TOKEN_TILE
    tile = lambda cols: pl.BlockSpec((tm, cols), lambda i: (i, 0))
    consts = list(extra_consts) + []
    return pl.pallas_call(
        kernel_fn,
        grid=(n // tm,),
        in_specs=[tile(a.shape[1]) for a in mixer_inputs]
        + [_const_spec(c.shape) for c in consts]
        + [tile(D_MODEL), _const_spec(wo.shape), _const_spec(ln1g.shape), _const_spec(ln1b.shape),
           _const_spec(w1.shape), _const_spec(w2.shape), _const_spec(ln2g.shape), _const_spec(ln2b.shape)],
        out_specs=tile(D_MODEL),
        out_shape=jax.ShapeDtypeStruct((n, D_MODEL), F32),
        compiler_params=_compiler_params(("parallel",)),
        name=kernel_fn.__name__.strip("_"),
    )(*mixer_inputs, *consts, x2, wo, ln1g, ln1b, w1, w2, ln2g, ln2b)


def _gla_in_kernel(x_ref, w_ref, q_ref, k_ref, v_ref, r_ref, lr_ref):
    h = jnp.dot(x_ref[...].astype(BF16), w_ref[...], preferred_element_type=F32)
    c1 = B_QK_COLS
    c2 = 2 * B_QK_COLS
    c3 = c2 + B_V_COLS
    c4 = c3 + B_V_COLS
    q_ref[...] = h[:, :c1] * (B_KEY_DIM ** -0.5)
    k_ref[...] = h[:, c1:c2]
    v_ref[...] = h[:, c2:c3].astype(BF16)
    r_ref[...] = h[:, c3:c4]
    lr_ref[...] = h[:, c4:].astype(BF16)


def _gla_in_proj(x2, w):
    n = x2.shape[0]
    tm = TOKEN_TILE
    tile = lambda cols: pl.BlockSpec((tm, cols), lambda i: (i, 0))
    return pl.pallas_call(
        _gla_in_kernel,
        grid=(n // tm,),
        in_specs=[tile(D_MODEL), _const_spec(w.shape)],
        out_specs=[tile(B_QK_COLS), tile(B_QK_COLS), tile(B_V_COLS), tile(B_V_COLS), tile(LANES)],
        out_shape=[
            jax.ShapeDtypeStruct((n, B_QK_COLS), F32),
            jax.ShapeDtypeStruct((n, B_QK_COLS), F32),
            jax.ShapeDtypeStruct((n, B_V_COLS), BF16),
            jax.ShapeDtypeStruct((n, B_V_COLS), F32),
            jax.ShapeDtypeStruct((n, LANES), BF16),
        ],
        compiler_params=_compiler_params(("parallel",)),
        name="gla_in_proj",
    )(x2, w)


def _log_gate(lr, w2, bias):
    z = jnp.dot(lr, w2, preferred_element_type=F32) + bias
    softplus_neg = jnp.maximum(-z, 0.0) + jnp.log1p(jnp.exp(-jnp.abs(z)))
    return -softplus_neg / B_GATE_TAU


def _chunk_cumsum(g, reverse):
    rows = g.shape[0]
    pos = lax.broadcasted_iota(jnp.int32, g.shape, 0) % B_CHUNK
    shift = 1
    while shift < B_CHUNK:
        if reverse:
            g = g + jnp.where(pos < B_CHUNK - shift, pltpu.roll(g, rows - shift, 0), 0.0)
        else:
            g = g + jnp.where(pos >= shift, pltpu.roll(g, shift, 0), 0.0)
        shift *= 2
    return g


def _gla_direction_chunk(c, reverse, q_ref, k_ref, v_ref, b_ref, o_ref, state_ref):
    rows = pl.ds(pl.multiple_of(c * B_CHUNK, B_CHUNK), B_CHUNK)
    b = b_ref[rows, :]
    edge = 0 if reverse else B_CHUNK - 1
    b_edge = b[edge:edge + 1, :]
    q = q_ref[rows, :]
    k = k_ref[rows, :]
    qe = (q * jnp.exp(b)).astype(BF16)
    ke = (k * jnp.exp(-b)).astype(BF16)
    kd = (k * jnp.exp(b_edge - b)).astype(BF16)
    decay = jnp.exp(b_edge)
    t_row = lax.broadcasted_iota(jnp.int32, (B_CHUNK, B_CHUNK), 0)
    t_col = lax.broadcasted_iota(jnp.int32, (B_CHUNK, B_CHUNK), 1)
    causal = (t_row <= t_col) if reverse else (t_row >= t_col)
    for h in range(B_HEADS):
        ks = slice(h * B_KEY_DIM, (h + 1) * B_KEY_DIM)
        vs = slice(h * B_VAL_DIM, (h + 1) * B_VAL_DIM)
        vh = v_ref[rows, vs]
        att = lax.dot_general(qe[:, ks], ke[:, ks], (((1,), (1,)), ((), ())), preferred_element_type=F32)
        att = jnp.where(causal, att, 0.0).astype(BF16)
        state = state_ref[h]
        o = jnp.dot(att, vh, preferred_element_type=F32)
        o = o + lax.dot_general(qe[:, ks], state.astype(BF16), (((1,), (1,)), ((), ())),
                                preferred_element_type=F32)
        update = lax.dot_general(vh, kd[:, ks], (((0,), (0,)), ((), ())), preferred_element_type=F32)
        state_ref[h] = state * decay[:, ks] + update
        o_ref[rows, vs] = o


def _gla_scan_kernel(qf_ref, kf_ref, vf_ref, lrf_ref, qb_ref, kb_ref, vb_ref, lrb_ref,
                     w2f_ref, gbf_ref, w2b_ref, gbb_ref, of_ref, ob_ref,
                     bf_ref, bb_ref, sf_ref, sb_ref):
    @pl.when(pl.program_id(1) == 0)
    def _():
        sf_ref[...] = jnp.zeros_like(sf_ref)
        sb_ref[...] = jnp.zeros_like(sb_ref)

    bf_ref[...] = _chunk_cumsum(_log_gate(lrf_ref[...], w2f_ref[...], gbf_ref[...]), reverse=False)
    bb_ref[...] = _chunk_cumsum(_log_gate(lrb_ref[...], w2b_ref[...], gbb_ref[...]), reverse=True)
    n_chunks = GLA_TILE // B_CHUNK

    def body(c, carry):
        _gla_direction_chunk(c, False, qf_ref, kf_ref, vf_ref, bf_ref, of_ref, sf_ref)
        _gla_direction_chunk(n_chunks - 1 - c, True, qb_ref, kb_ref, vb_ref, bb_ref, ob_ref, sb_ref)
        return carry

    lax.fori_loop(0, n_chunks, body, 0)


def _gla_scan(q, k, v, lr, w2f, gbf, w2b, gbb, batch, seq_len):
    n = q.shape[0]
    ts = GLA_TILE
    nt = seq_len // ts

    def fwd_map(b, i):
        return (b * nt + i, 0)

    def bwd_map(b, i):
        return (b * nt + nt - 1 - i, 0)

    def token_specs(index_map):
        return [pl.BlockSpec((ts, B_QK_COLS), index_map), pl.BlockSpec((ts, B_QK_COLS), index_map),
                pl.BlockSpec((ts, B_V_COLS), index_map), pl.BlockSpec((ts, LANES), index_map)]

    return pl.pallas_call(
        _gla_scan_kernel,
        grid=(batch, nt),
        in_specs=token_specs(fwd_map) + token_specs(bwd_map)
        + [_const_spec(w2f.shape), _const_spec(gbf.shape), _const_spec(w2b.shape), _const_spec(gbb.shape)],
        out_specs=[pl.BlockSpec((ts, B_V_COLS), fwd_map), pl.BlockSpec((ts, B_V_COLS), bwd_map)],
        out_shape=[jax.ShapeDtypeStruct((n, B_V_COLS), F32), jax.ShapeDtypeStruct((n, B_V_COLS), F32)],
        scratch_shapes=[
            pltpu.VMEM((ts, B_QK_COLS), F32),
            pltpu.VMEM((ts, B_QK_COLS), F32),
            pltpu.VMEM((B_HEADS, B_VAL_DIM, B_KEY_DIM), F32),
            pltpu.VMEM((B_HEADS, B_VAL_DIM, B_KEY_DIM), F32),
        ],
        compiler_params=_compiler_params(("parallel", "arbitrary")),
        name="gla_scan",
    )(q, k, v, lr, q, k, v, lr, w2f, gbf, w2b, gbb)


def _row(v):
    return v.reshape(1, -1).astype(F32)


def _pad_gate_weight(w2, first_row):
    out = jnp.zeros((LANES, w2.shape[1]), BF16)
    return lax.dynamic_update_slice(out, w2.astype(BF16), (first_row, 0))


def kernel(x, positions, attn_w_in, attn_sink, attn_w_out, gla_w_in, gla_gate_w2_fwd, gla_gate_b_fwd,
           gla_gate_w2_bwd, gla_gate_b_bwd, gla_norm_g, gla_w_out, mix_ln_g, mix_ln_b, mlp_w1, mlp_w2,
           mlp_ln_g, mlp_ln_b):
    batch, seq_len, d_model = x.shape
    n = batch * seq_len
    x2 = x.reshape(n, d_model)

    inv_freq = ROPE_THETA ** (-jnp.arange(0, A_HEAD_DIM, 2, dtype=F32) / A_HEAD_DIM)
    ang_t = inv_freq[:, None] * positions.astype(F32).reshape(1, n)
    q_cols = A_HEADS * A_HEAD_DIM
    k_cols = A_KV_HEADS * A_HEAD_DIM

    for i in range(DEPTH):
        j = i // 2
        w1 = mlp_w1[i].astype(BF16)
        w2 = mlp_w2[i].astype(BF16)
        tail = (_row(mix_ln_g[i]), _row(mix_ln_b[i]), w1, w2, _row(mlp_ln_g[i]), _row(mlp_ln_b[i]))
        if i % 2 == 0:
            w_in = attn_w_in[j].astype(BF16)
            wqv_t = jnp.concatenate([w_in[:, :q_cols], w_in[:, q_cols + k_cols:]], axis=1).T
            qt, kd, vt = _attn_in_proj(x2, ang_t, wqv_t, w_in[:, q_cols:q_cols + k_cols])
            o = _window_attention(qt, kd, vt, attn_sink[j].astype(F32), batch, seq_len)
            x2 = _post_mixer(_attn_post_kernel, [o], [], x2, attn_w_out[j].astype(BF16), *tail)
        else:
            w_in = gla_w_in[j]
            lr_cols = 2 * B_GATE_RANK
            w_in = jnp.pad(w_in, ((0, 0), (0, LANES - lr_cols))).astype(BF16)
            q, k, v, r, lr = _gla_in_proj(x2, w_in)
            o_f, o_b = _gla_scan(q, k, v, lr,
                                 _pad_gate_weight(gla_gate_w2_fwd[j], 0), _row(gla_gate_b_fwd[j]),
                                 _pad_gate_weight(gla_gate_w2_bwd[j], B_GATE_RANK), _row(gla_gate_b_bwd[j]),
                                 batch, seq_len)
            x2 = _post_mixer(_gla_post_kernel, [o_f, o_b, r], [_row(gla_norm_g[j])], x2,
                             gla_w_out[j].astype(BF16), *tail)
    return x2.reshape(batch, seq_len, d_model)
```

```python
import functools

import jax
import jax.numpy as jnp
from jax import lax
from jax.experimental import pallas as pl
from jax.experimental.pallas import tpu as pltpu

D_MODEL = 1024
DEPTH = 2

A_HEADS = 16
A_KV_HEADS = 4
A_HEAD_DIM = D_MODEL // A_HEADS
A_GROUP = A_HEADS // A_KV_HEADS
WINDOW = 128
ROPE_THETA = 10000.0

B_HEADS = 4
B_KEY_DIM = (D_MODEL // 2) // B_HEADS
B_VAL_DIM = D_MODEL // B_HEADS
B_GATE_RANK = 16
B_GATE_TAU = 16.0
B_CHUNK = 64
B_QK_COLS = B_HEADS * B_KEY_DIM
B_V_COLS = B_HEADS * B_VAL_DIM

D_FF = 4 * D_MODEL
DN_ALPHA = float((2 * DEPTH) ** 0.25)
LN_EPS = 1e-5
HEAD_NORM_EPS = 1e-6
LN_2 = 0.6931471805599453

LANES = 128
VMEM_LIMIT_BYTES = 56 * 1024 * 1024

TOKEN_TILE = 512
ATTN_Q_TILE = 512
GLA_TILE = 512
FF_CHUNK = 1024

BF16 = jnp.bfloat16
F32 = jnp.float32


def _const_spec(shape):
    zeros = (0,) * len(shape)
    return pl.BlockSpec(shape, lambda *_: zeros, pipeline_mode=pl.Buffered(1))


def _compiler_params(semantics, flags=None):
    return pltpu.CompilerParams(dimension_semantics=semantics, vmem_limit_bytes=VMEM_LIMIT_BYTES, flags=flags)


def _layer_norm(z, g, b):
    mu = jnp.mean(z, axis=-1, keepdims=True)
    zc = z - mu
    var = jnp.mean(zc * zc, axis=-1, keepdims=True)
    return zc * lax.rsqrt(var + LN_EPS) * g + b


def _lane_index(shape):
    return lax.broadcasted_iota(jnp.int32, shape, len(shape) - 1)


def _attn_in_kernel(x_ref, ang_ref, wqv_ref, wk_ref, qt_ref, k_ref, vt_ref):
    xb = x_ref[...].astype(BF16)
    half = A_HEAD_DIM // 2
    q_cols = A_HEADS * A_HEAD_DIM
    kv_cols = A_KV_HEADS * A_HEAD_DIM
    cos_t = jnp.cos(ang_ref[...])
    sin_t = jnp.sin(ang_ref[...])

    hqv = lax.dot_general(wqv_ref[...], xb, (((1,), (1,)), ((), ())), preferred_element_type=F32)
    scale = A_HEAD_DIM ** -0.5
    for h in range(A_HEADS):
        t1 = hqv[h * A_HEAD_DIM:h * A_HEAD_DIM + half, :]
        t2 = hqv[h * A_HEAD_DIM + half:(h + 1) * A_HEAD_DIM, :]
        qt_ref[h * A_HEAD_DIM:h * A_HEAD_DIM + half, :] = ((t1 * cos_t - t2 * sin_t) * scale).astype(BF16)
        qt_ref[h * A_HEAD_DIM + half:(h + 1) * A_HEAD_DIM, :] = ((t2 * cos_t + t1 * sin_t) * scale).astype(BF16)
    vt_ref[...] = hqv[q_cols:, :].astype(BF16)

    cos = jnp.concatenate([cos_t] * (LANES // half), axis=0).T
    sin = jnp.concatenate([sin_t] * (LANES // half), axis=0).T
    lane = _lane_index(cos.shape)
    first_half = (lane % A_HEAD_DIM) < half
    sin_signed = jnp.where(first_half, -sin, sin)
    low_head = lane < A_HEAD_DIM
    hk = jnp.dot(xb, wk_ref[...], preferred_element_type=F32)
    for j in range(kv_cols // LANES):
        t = hk[:, j * LANES:(j + 1) * LANES]
        partner = jnp.where(first_half, pltpu.roll(t, LANES - half, 1), pltpu.roll(t, half, 1))
        t = t * cos + partner * sin_signed
        swapped = pltpu.roll(t, A_HEAD_DIM, 1)
        k_ref[:, (2 * j) * LANES:(2 * j + 1) * LANES] = jnp.where(low_head, t, swapped).astype(BF16)
        k_ref[:, (2 * j + 1) * LANES:(2 * j + 2) * LANES] = jnp.where(low_head, swapped, t).astype(BF16)


def _attn_in_proj(x2, ang_t, wqv_t, wk):
    n = x2.shape[0]
    tm = TOKEN_TILE
    q_cols = A_HEADS * A_HEAD_DIM
    kv_cols = A_KV_HEADS * A_HEAD_DIM
    k_dup = A_KV_HEADS * LANES
    return pl.pallas_call(
        _attn_in_kernel,
        grid=(n // tm,),
        in_specs=[
            pl.BlockSpec((tm, D_MODEL), lambda i: (i, 0)),
            pl.BlockSpec((ang_t.shape[0], tm), lambda i: (0, i)),
            _const_spec(wqv_t.shape),
            _const_spec(wk.shape),
        ],
        out_specs=[
            pl.BlockSpec((q_cols, tm), lambda i: (0, i)),
            pl.BlockSpec((tm, k_dup), lambda i: (i, 0)),
            pl.BlockSpec((kv_cols, tm), lambda i: (0, i)),
        ],
        out_shape=[
            jax.ShapeDtypeStruct((q_cols, n), BF16),
            jax.ShapeDtypeStruct((n, k_dup), BF16),
            jax.ShapeDtypeStruct((kv_cols, n), BF16),
        ],
        compiler_params=_compiler_params(("parallel",)),
        name="attn_in_proj",
    )(x2, ang_t, wqv_t, wk)


def _window_attn_kernel(sink_ref, qt_ref, kp_ref, km_ref, kn_ref, vp_ref, vm_ref, vn_ref, o_ref,
                        kbuf, vbuf):
    tq = ATTN_Q_TILE
    blk = WINDOW
    span = 3 * blk
    i = pl.program_id(1)
    last_tile = pl.num_programs(1) - 1
    kbuf[0:blk, :] = kp_ref[...]
    kbuf[blk:blk + tq, :] = km_ref[...]
    kbuf[blk + tq:, :] = kn_ref[...]
    vbuf[:, 0:blk] = vp_ref[...]
    vbuf[:, blk:blk + tq] = vm_ref[...]
    vbuf[:, blk + tq:] = vn_ref[...]

    pair_shape = (blk, 2 * LANES)
    key = lax.broadcasted_iota(jnp.int32, pair_shape, 0)
    query = _lane_index(pair_shape) % blk
    neg_inf = jnp.float32(-jnp.inf)
    prev_bias = jnp.where(key >= query, 0.0, neg_inf)
    next_bias = jnp.where(key <= query, 0.0, neg_inf)
    first_prev_bias = jnp.where(i == 0, neg_inf, prev_bias)
    last_next_bias = jnp.where(i == last_tile, neg_inf, next_bias)
    low_rows = lax.broadcasted_iota(jnp.int32, (LANES, blk), 0) < A_HEAD_DIM
    zero = jnp.zeros((LANES, blk), BF16)

    n_blocks = tq // blk
    units = [(jj, g, pr) for jj in range(n_blocks) for g in range(A_KV_HEADS) for pr in range(A_GROUP // 2)]

    def pair_lanes(g, pr):
        return slice((g * 2 + pr) * LANES, (g * 2 + pr + 1) * LANES)

    def scores(jj, g, pr):
        kk = kbuf[jj * blk:jj * blk + span, g * LANES:(g + 1) * LANES]
        qt = qt_ref[pair_lanes(g, pr), jj * blk:(jj + 1) * blk]
        w = jnp.concatenate([jnp.where(low_rows, qt, zero), jnp.where(low_rows, zero, qt)], axis=1)
        return jnp.dot(kk, w, preferred_element_type=F32)

    def softmax(s, jj, g, pr):
        s_prev = s[0:blk] + (first_prev_bias if jj == 0 else prev_bias)
        s_mid = s[blk:2 * blk]
        s_next = s[2 * blk:] + (last_next_bias if jj == n_blocks - 1 else next_bias)
        head = g * A_GROUP + 2 * pr
        sink = jnp.concatenate([jnp.full((1, blk), sink_ref[head], F32),
                                jnp.full((1, blk), sink_ref[head + 1], F32)], axis=1)
        m = jnp.maximum(jnp.maximum(jnp.max(s_prev, axis=0, keepdims=True),
                                    jnp.max(s_mid, axis=0, keepdims=True)),
                        jnp.maximum(jnp.max(s_next, axis=0, keepdims=True), sink))
        p = jnp.concatenate([jnp.exp(s_prev - m), jnp.exp(s_mid - m), jnp.exp(s_next - m)], axis=0)
        denom = jnp.sum(p, axis=0, keepdims=True) + jnp.exp(sink - m)
        return p.astype(BF16), 1.0 / denom

    def weighted_values(p, inv_denom, jj, g, pr):
        vt = vbuf[g * A_HEAD_DIM:(g + 1) * A_HEAD_DIM, jj * blk:jj * blk + span]
        ot = jnp.dot(vt, p, preferred_element_type=F32) * inv_denom
        both = jnp.concatenate([ot[:, 0:blk], ot[:, blk:]], axis=0)
        o_ref[jj * blk:(jj + 1) * blk, pair_lanes(g, pr)] = both.T.astype(BF16)

    s_stage = {}
    p_stage = {}
    for n in range(len(units) + 2):
        if n < len(units):
            s_stage[n] = scores(*units[n])
        if 0 <= n - 1 < len(units):
            p_stage[n - 1] = softmax(s_stage.pop(n - 1), *units[n - 1])
        if 0 <= n - 2 < len(units):
            weighted_values(*p_stage.pop(n - 2), *units[n - 2])


def _window_attention(qt, kd, vt, sink, batch, seq_len):
    q_cols, n = qt.shape
    v_cols = vt.shape[0]
    tq = ATTN_Q_TILE
    blk = WINDOW
    nt = seq_len // tq
    per_tile = tq // blk
    nblk = seq_len // blk
    k_dup = kd.shape[1]

    def main_idx(b, i):
        return b * nt + i

    def prev_idx(b, i):
        return b * nblk + jnp.maximum(i * per_tile - 1, 0)

    def next_idx(b, i):
        return b * nblk + jnp.minimum((i + 1) * per_tile, nblk - 1)

    k_specs = [pl.BlockSpec((blk, k_dup), lambda b, i: (prev_idx(b, i), 0)),
               pl.BlockSpec((tq, k_dup), lambda b, i: (main_idx(b, i), 0)),
               pl.BlockSpec((blk, k_dup), lambda b, i: (next_idx(b, i), 0))]
    v_specs = [pl.BlockSpec((v_cols, blk), lambda b, i: (0, prev_idx(b, i))),
               pl.BlockSpec((v_cols, tq), lambda b, i: (0, main_idx(b, i))),
               pl.BlockSpec((v_cols, blk), lambda b, i: (0, next_idx(b, i)))]
    return pl.pallas_call(
        _window_attn_kernel,
        grid=(batch, nt),
        in_specs=[pl.BlockSpec(memory_space=pltpu.SMEM),
                  pl.BlockSpec((q_cols, tq), lambda b, i: (0, main_idx(b, i)))] + k_specs + v_specs,
        out_specs=pl.BlockSpec((tq, q_cols), lambda b, i: (main_idx(b, i), 0)),
        out_shape=jax.ShapeDtypeStruct((n, q_cols), BF16),
        scratch_shapes=[pltpu.VMEM((tq + 2 * blk, k_dup), BF16), pltpu.VMEM((v_cols, tq + 2 * blk), BF16)],
        compiler_params=_compiler_params(("parallel", "parallel")),
        name="window_attn",
    )(sink, qt, kd, kd, kd, vt, vt, vt)


def _residual_mlp_tail(mixer_out, x_ref, wo_ref, ln1g_ref, ln1b_ref, w1_ref, w2_ref, ln2g_ref, ln2b_ref,
                       out_ref):
    tm = x_ref.shape[0]
    halves = [slice(0, tm // 2), slice(tm // 2, tm)]
    n_ff = D_FF // FF_CHUNK

    def project(rows):
        return jnp.dot(mixer_out(rows), wo_ref[...], preferred_element_type=F32)

    def norm1(y, rows):
        x1 = _layer_norm(DN_ALPHA * x_ref[rows, :] + y, ln1g_ref[...], ln1b_ref[...])
        return x1, x1.astype(BF16)

    def mlp_chunk(x1b, acc, c):
        hc = jnp.dot(x1b, w1_ref[:, c * FF_CHUNK:(c + 1) * FF_CHUNK], preferred_element_type=F32)
        hc = jnp.square(jnp.maximum(hc, 0.0)).astype(BF16)
        part = jnp.dot(hc, w2_ref[c * FF_CHUNK:(c + 1) * FF_CHUNK, :], preferred_element_type=F32)
        return part if acc is None else acc + part

    def norm2(x1, acc, rows):
        out_ref[rows, :] = _layer_norm(DN_ALPHA * x1 + acc, ln2g_ref[...], ln2b_ref[...])

    a, b = halves
    x1_a, x1b_a = norm1(project(a), a)
    y_b = project(b)
    acc_a = mlp_chunk(x1b_a, None, 0)
    x1_b, x1b_b = norm1(y_b, b)
    for c in range(1, n_ff):
        acc_a = mlp_chunk(x1b_a, acc_a, c)
    acc_b = mlp_chunk(x1b_b, None, 0)
    norm2(x1_a, acc_a, a)
    for c in range(1, n_ff):
        acc_b = mlp_chunk(x1b_b, acc_b, c)
    norm2(x1_b, acc_b, b)


def _attn_post_kernel(o_ref, x_ref, wo_ref, ln1g_ref, ln1b_ref, w1_ref, w2_ref, ln2g_ref, ln2b_ref, out_ref):
    _residual_mlp_tail(lambda rows: o_ref[rows, :], x_ref, wo_ref, ln1g_ref, ln1b_ref, w1_ref, w2_ref,
                       ln2g_ref, ln2b_ref, out_ref)


def _gla_post_kernel(of_ref, ob_ref, r_ref, ng_ref, x_ref, wo_ref, ln1g_ref, ln1b_ref, w1_ref, w2_ref,
                     ln2g_ref, ln2b_ref, out_ref):
    def mixer_out(rows):
        o = of_ref[rows, :] + ob_ref[rows, :]
        r = r_ref[rows, :]
        gate = r * (1.0 / (1.0 + jnp.exp(-r)))
        ng = ng_ref[...]
        heads = []
        for h in range(B_HEADS):
            oh = o[:, h * B_VAL_DIM:(h + 1) * B_VAL_DIM]
            oh = oh * lax.rsqrt(jnp.mean(oh * oh, axis=-1, keepdims=True) + HEAD_NORM_EPS)
            heads.append((oh * ng * gate[:, h * B_VAL_DIM:(h + 1) * B_VAL_DIM]).astype(BF16))
        return jnp.concatenate(heads, axis=-1)

    _residual_mlp_tail(mixer_out, x_ref, wo_ref, ln1g_ref, ln1b_ref, w1_ref, w2_ref, ln2g_ref, ln2b_ref,
                       out_ref)


def _post_mixer(kernel_fn, mixer_inputs, extra_consts, x2, wo, ln1g, ln1b, w1, w2, ln2g, ln2b):
    n = x2.shape[0]
    tm = TOKEN_TILE
    tile = lambda cols: pl.BlockSpec((tm, cols), lambda i: (i, 0))
    consts = list(extra_consts) + []
    return pl.pallas_call(
        kernel_fn,
        grid=(n // tm,),
        in_specs=[tile(a.shape[1]) for a in mixer_inputs]
        + [_const_spec(c.shape) for c in consts]
        + [tile(D_MODEL), _const_spec(wo.shape), _const_spec(ln1g.shape), _const_spec(ln1b.shape),
           _const_spec(w1.shape), _const_spec(w2.shape), _const_spec(ln2g.shape), _const_spec(ln2b.shape)],
        out_specs=tile(D_MODEL),
        out_shape=jax.ShapeDtypeStruct((n, D_MODEL), F32),
        compiler_params=_compiler_params(("parallel",)),
        name=kernel_fn.__name__.strip("_"),
    )(*mixer_inputs, *consts, x2, wo, ln1g, ln1b, w1, w2, ln2g, ln2b)


def _gla_in_kernel(x_ref, w_ref, q_ref, k_ref, v_ref, r_ref, lr_ref):
    h = jnp.dot(x_ref[...].astype(BF16), w_ref[...], preferred_element_type=F32)
    c1 = B_QK_COLS
    c2 = 2 * B_QK_COLS
    c3 = c2 + B_V_COLS
    c4 = c3 + B_V_COLS
    q_ref[...] = h[:, :c1] * (B_KEY_DIM ** -0.5)
    k_ref[...] = h[:, c1:c2]
    v_ref[...] = h[:, c2:c3].astype(BF16)
    r_ref[...] = h[:, c3:c4]
    lr_ref[...] = h[:, c4:].astype(BF16)


def _gla_in_proj(x2, w):
    n = x2.shape[0]
    tm = TOKEN_TILE
    tile = lambda cols: pl.BlockSpec((tm, cols), lambda i: (i, 0))
    return pl.pallas_call(
        _gla_in_kernel,
        grid=(n // tm,),
        in_specs=[tile(D_MODEL), _const_spec(w.shape)],
        out_specs=[tile(B_QK_COLS), tile(B_QK_COLS), tile(B_V_COLS), tile(B_V_COLS), tile(LANES)],
        out_shape=[
            jax.ShapeDtypeStruct((n, B_QK_COLS), F32),
            jax.ShapeDtypeStruct((n, B_QK_COLS), F32),
            jax.ShapeDtypeStruct((n, B_V_COLS), BF16),
            jax.ShapeDtypeStruct((n, B_V_COLS), F32),
            jax.ShapeDtypeStruct((n, LANES), BF16),
        ],
        compiler_params=_compiler_params(("parallel",)),
        name="gla_in_proj",
    )(x2, w)


def _log_gate(lr, w2, bias):
    z = jnp.dot(lr, w2, preferred_element_type=F32) + bias
    log_term = jnp.log2(1.0 + jnp.exp(-jnp.abs(z)))
    return jnp.minimum(z, 0.0) * (1.0 / B_GATE_TAU) - log_term * (LN_2 / B_GATE_TAU)


def _chunk_cumsum(g, reverse):
    hi = g.astype(BF16)
    rest = g - hi.astype(F32)
    mid = rest.astype(BF16)
    lo = (rest - mid.astype(F32)).astype(BF16)
    t_row = lax.broadcasted_iota(jnp.int32, (B_CHUNK, 3 * B_CHUNK), 0)
    t_col = lax.broadcasted_iota(jnp.int32, (B_CHUNK, 3 * B_CHUNK), 1) % B_CHUNK
    keep = (t_row <= t_col) if reverse else (t_row >= t_col)
    tri = jnp.where(keep, 1.0, 0.0).astype(BF16)
    out = []
    for c in range(g.shape[0] // B_CHUNK):
        rows = slice(c * B_CHUNK, (c + 1) * B_CHUNK)
        terms = jnp.concatenate([hi[rows], mid[rows], lo[rows]], axis=0)
        out.append(jnp.dot(tri, terms, preferred_element_type=F32))
    return jnp.concatenate(out, axis=0)


def _gla_scan_kernel(qf_ref, kf_ref, vf_ref, lrf_ref, qb_ref, kb_ref, vb_ref, lrb_ref,
                     w2f_ref, gbf_ref, w2b_ref, gbb_ref, of_ref, ob_ref,
                     bf_ref, bb_ref, d_ref, sf_ref, sb_ref):
    @pl.when(pl.program_id(1) == 0)
    def _():
        sf_ref[...] = jnp.zeros_like(sf_ref)
        sb_ref[...] = jnp.zeros_like(sb_ref)

    n_chunks = GLA_TILE // B_CHUNK
    pair_k = 2 * B_KEY_DIM

    def chunk_totals(lr_ref, w2_ref, gb_ref, b_ref, reverse):
        b = _chunk_cumsum(_log_gate(lr_ref[...], w2_ref[...], gb_ref[...]), reverse)
        b_ref[...] = b
        edge = 0 if reverse else B_CHUNK - 1
        return [b[c * B_CHUNK + edge:c * B_CHUNK + edge + 1, :] for c in range(n_chunks)]

    totals = (chunk_totals(lrf_ref, w2f_ref, gbf_ref, bf_ref, False)
              + chunk_totals(lrb_ref, w2b_ref, gbb_ref, bb_ref, True))
    d_ref[...] = jnp.exp(jnp.concatenate(totals, axis=0)).T

    directions = ((False, qf_ref, kf_ref, vf_ref, bf_ref, 0, of_ref, sf_ref),
                  (True, qb_ref, kb_ref, vb_ref, bb_ref, n_chunks, ob_ref, sb_ref))
    t_row = lax.broadcasted_iota(jnp.int32, (B_CHUNK, 2 * B_CHUNK), 0)
    t_col = _lane_index((B_CHUNK, 2 * B_CHUNK)) % B_CHUNK
    first_head_lanes = _lane_index((B_CHUNK, pair_k)) < B_KEY_DIM
    zero_k = jnp.zeros((B_CHUNK, pair_k), BF16)
    zero_v = jnp.zeros((B_CHUNK, B_VAL_DIM), BF16)
    units = [(c, d, hp) for c in range(n_chunks) for d in range(2) for hp in range(B_HEADS // 2)]

    def chunk_rows(c, reverse):
        cc = n_chunks - 1 - c if reverse else c
        return cc, slice(cc * B_CHUNK, (cc + 1) * B_CHUNK)

    def prepare(c, d, hp):
        reverse, q_ref, k_ref, v_ref, b_ref = directions[d][:5]
        _, rows = chunk_rows(c, reverse)
        cols = slice(hp * pair_k, (hp + 1) * pair_k)
        b = b_ref[rows, cols]
        edge = 0 if reverse else B_CHUNK - 1
        b_edge = b[edge:edge + 1, :]
        k = k_ref[rows, cols]
        qe = (q_ref[rows, cols] * jnp.exp(b)).astype(BF16)
        ke = (k * jnp.exp(-b)).astype(BF16)
        kd = (k * jnp.exp(b_edge - b)).astype(BF16)
        ke_bd = jnp.concatenate([jnp.where(first_head_lanes, ke, zero_k),
                                 jnp.where(first_head_lanes, zero_k, ke)], axis=0)
        att = lax.dot_general(qe, ke_bd, (((1,), (1,)), ((), ())), preferred_element_type=F32)
        causal = (t_row <= t_col) if reverse else (t_row >= t_col)
        att = jnp.where(causal, att, 0.0).astype(BF16)
        updates = []
        for hh in range(2):
            h = 2 * hp + hh
            vh = v_ref[rows, h * B_VAL_DIM:(h + 1) * B_VAL_DIM]
            updates.append(lax.dot_general(kd[:, hh * B_KEY_DIM:(hh + 1) * B_KEY_DIM], vh,
                                           (((0,), (0,)), ((), ())), preferred_element_type=F32))
        return qe, att, updates

    def apply(prepared, c, d, hp):
        reverse, _, _, v_ref, _, d_col0, o_ref, s_ref = directions[d]
        qe, att, updates = prepared
        cc, rows = chunk_rows(c, reverse)
        for hh in range(2):
            h = 2 * hp + hh
            vs = slice(h * B_VAL_DIM, (h + 1) * B_VAL_DIM)
            vh = v_ref[rows, vs]
            state = s_ref[h]
            lhs = jnp.concatenate([qe[:, hh * B_KEY_DIM:(hh + 1) * B_KEY_DIM], att], axis=1)
            rhs = jnp.concatenate([state.astype(BF16)] + ([vh, zero_v] if hh == 0 else [zero_v, vh]), axis=0)
            o_ref[rows, vs] = jnp.dot(lhs, rhs, preferred_element_type=F32)
            decay = d_ref[h * B_KEY_DIM:(h + 1) * B_KEY_DIM, d_col0 + cc:d_col0 + cc + 1]
            s_ref[h] = state * decay + updates[hh]

    prepared = {}
    for n in range(len(units) + 1):
        if n < len(units):
            prepared[n] = prepare(*units[n])
        if n >= 1:
            apply(prepared.pop(n - 1), *units[n - 1])


def _gla_scan(q, k, v, lr, w2f, gbf, w2b, gbb, batch, seq_len):
    n = q.shape[0]
    ts = GLA_TILE
    nt = seq_len // ts

    def fwd_map(b, i):
        return (b * nt + i, 0)

    def bwd_map(b, i):
        return (b * nt + nt - 1 - i, 0)

    def token_specs(index_map):
        return [pl.BlockSpec((ts, B_QK_COLS), index_map), pl.BlockSpec((ts, B_QK_COLS), index_map),
                pl.BlockSpec((ts, B_V_COLS), index_map), pl.BlockSpec((ts, LANES), index_map)]

    return pl.pallas_call(
        _gla_scan_kernel,
        grid=(batch, nt),
        in_specs=token_specs(fwd_map) + token_specs(bwd_map)
        + [_const_spec(w2f.shape), _const_spec(gbf.shape), _const_spec(w2b.shape), _const_spec(gbb.shape)],
        out_specs=[pl.BlockSpec((ts, B_V_COLS), fwd_map), pl.BlockSpec((ts, B_V_COLS), bwd_map)],
        out_shape=[jax.ShapeDtypeStruct((n, B_V_COLS), F32), jax.ShapeDtypeStruct((n, B_V_COLS), F32)],
        scratch_shapes=[
            pltpu.VMEM((ts, B_QK_COLS), F32),
            pltpu.VMEM((ts, B_QK_COLS), F32),
            pltpu.VMEM((B_QK_COLS, 2 * (ts // B_CHUNK)), F32),
            pltpu.VMEM((B_HEADS, B_KEY_DIM, B_VAL_DIM), F32),
            pltpu.VMEM((B_HEADS, B_KEY_DIM, B_VAL_DIM), F32),
        ],
        compiler_params=_compiler_params(("parallel", "arbitrary")),
        name="gla_scan",
    )(q, k, v, lr, q, k, v, lr, w2f, gbf, w2b, gbb)


def _row(v):
    return v.reshape(1, -1).astype(F32)


def _pad_gate_weight(w2, first_row):
    out = jnp.zeros((LANES, w2.shape[1]), BF16)
    return lax.dynamic_update_slice(out, w2.astype(BF16), (first_row, 0))


def kernel(x, positions, attn_w_in, attn_sink, attn_w_out, gla_w_in, gla_gate_w2_fwd, gla_gate_b_fwd,
           gla_gate_w2_bwd, gla_gate_b_bwd, gla_norm_g, gla_w_out, mix_ln_g, mix_ln_b, mlp_w1, mlp_w2,
           mlp_ln_g, mlp_ln_b):
    batch, seq_len, d_model = x.shape
    n = batch * seq_len
    x2 = x.reshape(n, d_model)

    inv_freq = ROPE_THETA ** (-jnp.arange(0, A_HEAD_DIM, 2, dtype=F32) / A_HEAD_DIM)
    ang_t = inv_freq[:, None] * positions.astype(F32).reshape(1, n)
    q_cols = A_HEADS * A_HEAD_DIM
    k_cols = A_KV_HEADS * A_HEAD_DIM

    for i in range(DEPTH):
        j = i // 2
        w1 = mlp_w1[i].astype(BF16)
        w2 = mlp_w2[i].astype(BF16)
        tail = (_row(mix_ln_g[i]), _row(mix_ln_b[i]), w1, w2, _row(mlp_ln_g[i]), _row(mlp_ln_b[i]))
        if i % 2 == 0:
            w_in = attn_w_in[j].astype(BF16)
            wqv_t = jnp.concatenate([w_in[:, :q_cols], w_in[:, q_cols + k_cols:]], axis=1).T
            qt, kd, vt = _attn_in_proj(x2, ang_t, wqv_t, w_in[:, q_cols:q_cols + k_cols])
            o = _window_attention(qt, kd, vt, attn_sink[j].astype(F32), batch, seq_len)
            x2 = _post_mixer(_attn_post_kernel, [o], [], x2, attn_w_out[j].astype(BF16), *tail)
        else:
            w_in = gla_w_in[j]
            lr_cols = 2 * B_GATE_RANK
            w_in = jnp.pad(w_in, ((0, 0), (0, LANES - lr_cols))).astype(BF16)
            q, k, v, r, lr = _gla_in_proj(x2, w_in)
            o_f, o_b = _gla_scan(q, k, v, lr,
                                 _pad_gate_weight(gla_gate_w2_fwd[j], 0), _row(gla_gate_b_fwd[j]),
                                 _pad_gate_weight(gla_gate_w2_bwd[j], B_GATE_RANK), _row(gla_gate_b_bwd[j]),
                                 batch, seq_len)
            x2 = _post_mixer(_gla_post_kernel, [o_f, o_b, r], [_row(gla_norm_g[j])], x2,
                             gla_w_out[j].astype(BF16), *tail)
    return x2.reshape(batch, seq_len, d_model)
```

```python
import functools

import jax
import jax.numpy as jnp
from jax import lax
from jax.experimental import pallas as pl
from jax.experimental.pallas import tpu as pltpu

D_MODEL = 1024
DEPTH = 2

A_HEADS = 16
A_KV_HEADS = 4
A_HEAD_DIM = D_MODEL // A_HEADS
A_GROUP = A_HEADS // A_KV_HEADS
WINDOW = 128
ROPE_THETA = 10000.0

B_HEADS = 4
B_KEY_DIM = (D_MODEL // 2) // B_HEADS
B_VAL_DIM = D_MODEL // B_HEADS
B_GATE_RANK = 16
B_GATE_TAU = 16.0
B_CHUNK = 64
B_QK_COLS = B_HEADS * B_KEY_DIM
B_V_COLS = B_HEADS * B_VAL_DIM

D_FF = 4 * D_MODEL
DN_ALPHA = float((2 * DEPTH) ** 0.25)
LN_EPS = 1e-5
HEAD_NORM_EPS = 1e-6
LN_2 = 0.6931471805599453
LOG2_E = 1.4426950408889634

LANES = 128
BF16_ROWS = 16
V_ROWS = A_HEAD_DIM + BF16_ROWS
VMEM_LIMIT_BYTES = 56 * 1024 * 1024

TOKEN_TILE = 512
ATTN_Q_TILE = 1024
GLA_TILE = TOKEN_TILE
FF_CHUNK = 1024

BF16 = jnp.bfloat16
F32 = jnp.float32


def _const_spec(shape):
    zeros = (0,) * len(shape)
    return pl.BlockSpec(shape, lambda *_: zeros, pipeline_mode=pl.Buffered(1))


def _compiler_params(semantics, flags=None):
    return pltpu.CompilerParams(dimension_semantics=semantics, vmem_limit_bytes=VMEM_LIMIT_BYTES, flags=flags)


def _layer_norm(z, g, b):
    mu = jnp.mean(z, axis=-1, keepdims=True)
    zc = z - mu
    var = jnp.mean(zc * zc, axis=-1, keepdims=True)
    return zc * lax.rsqrt(var + LN_EPS) * g + b


def _lane_index(shape):
    return lax.broadcasted_iota(jnp.int32, shape, len(shape) - 1)


def _attn_in_kernel(x_ref, ang_ref, wqv_ref, wk_ref, qt_ref, k_ref, vt_ref):
    xb = x_ref[...].astype(BF16)
    half = A_HEAD_DIM // 2
    q_cols = A_HEADS * A_HEAD_DIM
    kv_cols = A_KV_HEADS * A_HEAD_DIM
    cos_t = jnp.cos(ang_ref[...])
    sin_t = jnp.sin(ang_ref[...])

    hqv = lax.dot_general(wqv_ref[...], xb, (((1,), (1,)), ((), ())), preferred_element_type=F32)
    scale = A_HEAD_DIM ** -0.5 * LOG2_E
    for h in range(A_HEADS):
        t1 = hqv[h * A_HEAD_DIM:h * A_HEAD_DIM + half, :]
        t2 = hqv[h * A_HEAD_DIM + half:(h + 1) * A_HEAD_DIM, :]
        qt_ref[h * A_HEAD_DIM:h * A_HEAD_DIM + half, :] = ((t1 * cos_t - t2 * sin_t) * scale).astype(BF16)
        qt_ref[h * A_HEAD_DIM + half:(h + 1) * A_HEAD_DIM, :] = ((t2 * cos_t + t1 * sin_t) * scale).astype(BF16)
    vt_ref[...] = hqv[q_cols:, :].astype(BF16)

    cos = jnp.concatenate([cos_t] * (LANES // half), axis=0).T
    sin = jnp.concatenate([sin_t] * (LANES // half), axis=0).T
    lane = _lane_index(cos.shape)
    first_half = (lane % A_HEAD_DIM) < half
    sin_signed = jnp.where(first_half, -sin, sin)
    low_head = lane < A_HEAD_DIM
    hk = jnp.dot(xb, wk_ref[...], preferred_element_type=F32)
    for j in range(kv_cols // LANES):
        t = hk[:, j * LANES:(j + 1) * LANES]
        partner = jnp.where(first_half, pltpu.roll(t, LANES - half, 1), pltpu.roll(t, half, 1))
        t = t * cos + partner * sin_signed
        swapped = pltpu.roll(t, A_HEAD_DIM, 1)
        k_ref[:, (2 * j) * LANES:(2 * j + 1) * LANES] = jnp.where(low_head, t, swapped).astype(BF16)
        k_ref[:, (2 * j + 1) * LANES:(2 * j + 2) * LANES] = jnp.where(low_head, swapped, t).astype(BF16)


def _attn_in_proj(x2, ang_t, wqv_t, wk):
    n = x2.shape[0]
    tm = TOKEN_TILE
    q_cols = A_HEADS * A_HEAD_DIM
    kv_cols = A_KV_HEADS * A_HEAD_DIM
    k_dup = A_KV_HEADS * LANES
    return pl.pallas_call(
        _attn_in_kernel,
        grid=(n // tm,),
        in_specs=[
            pl.BlockSpec((tm, D_MODEL), lambda i: (i, 0)),
            pl.BlockSpec((ang_t.shape[0], tm), lambda i: (0, i)),
            _const_spec(wqv_t.shape),
            _const_spec(wk.shape),
        ],
        out_specs=[
            pl.BlockSpec((q_cols, tm), lambda i: (0, i)),
            pl.BlockSpec((tm, k_dup), lambda i: (i, 0)),
            pl.BlockSpec((kv_cols, tm), lambda i: (0, i)),
        ],
        out_shape=[
            jax.ShapeDtypeStruct((q_cols, n), BF16),
            jax.ShapeDtypeStruct((n, k_dup), BF16),
            jax.ShapeDtypeStruct((kv_cols, n), BF16),
        ],
        compiler_params=_compiler_params(("parallel",)),
        name="attn_in_proj",
    )(x2, ang_t, wqv_t, wk)


def _window_attn_kernel(sink_ref, qt_ref, kp_ref, km_ref, kn_ref, vp_ref, vm_ref, vn_ref, o_ref,
                        kbuf, vbuf):
    tq = ATTN_Q_TILE
    blk = WINDOW
    span = 3 * blk
    i = pl.program_id(1)
    last_tile = pl.num_programs(1) - 1
    kbuf[0:blk, :] = kp_ref[...]
    kbuf[blk:blk + tq, :] = km_ref[...]
    kbuf[blk + tq:, :] = kn_ref[...]
    for g in range(A_KV_HEADS):
        src = slice(g * A_HEAD_DIM, (g + 1) * A_HEAD_DIM)
        dst = slice(g * V_ROWS, g * V_ROWS + A_HEAD_DIM)
        vbuf[dst, 0:blk] = vp_ref[src, :]
        vbuf[dst, blk:blk + tq] = vm_ref[src, :]
        vbuf[dst, blk + tq:] = vn_ref[src, :]
        vbuf[g * V_ROWS + A_HEAD_DIM:(g + 1) * V_ROWS, :] = jnp.ones((BF16_ROWS, tq + 2 * blk), BF16)

    pair_shape = (blk, 2 * LANES)
    key = lax.broadcasted_iota(jnp.int32, pair_shape, 0)
    query = _lane_index(pair_shape) % blk
    neg_inf = jnp.float32(-jnp.inf)
    prev_bias = jnp.where(key >= query, 0.0, neg_inf)
    next_bias = jnp.where(key <= query, 0.0, neg_inf)
    first_prev_bias = jnp.where(i == 0, neg_inf, prev_bias)
    last_next_bias = jnp.where(i == last_tile, neg_inf, next_bias)
    low_rows = lax.broadcasted_iota(jnp.int32, (LANES, blk), 0) < A_HEAD_DIM
    zero = jnp.zeros((LANES, blk), BF16)

    n_blocks = tq // blk
    units = [(jj, g, pr) for jj in range(n_blocks) for g in range(A_KV_HEADS) for pr in range(A_GROUP // 2)]

    def pair_lanes(g, pr):
        return slice((g * 2 + pr) * LANES, (g * 2 + pr + 1) * LANES)

    def scores(jj, g, pr):
        kk = kbuf[jj * blk:jj * blk + span, g * LANES:(g + 1) * LANES]
        qt = qt_ref[pair_lanes(g, pr), jj * blk:(jj + 1) * blk]
        w = jnp.concatenate([jnp.where(low_rows, qt, zero), jnp.where(low_rows, zero, qt)], axis=1)
        return jnp.dot(kk, w, preferred_element_type=F32)

    def softmax(s, jj, g, pr):
        s_prev = s[0:blk] + (first_prev_bias if jj == 0 else prev_bias)
        s_mid = s[blk:2 * blk]
        s_next = s[2 * blk:] + (last_next_bias if jj == n_blocks - 1 else next_bias)
        head = g * A_GROUP + 2 * pr
        sink = jnp.concatenate([jnp.full((1, blk), sink_ref[head] * LOG2_E, F32),
                                jnp.full((1, blk), sink_ref[head + 1] * LOG2_E, F32)], axis=1)
        m = jnp.maximum(jnp.maximum(jnp.max(s_prev, axis=0, keepdims=True),
                                    jnp.max(s_mid, axis=0, keepdims=True)),
                        jnp.maximum(jnp.max(s_next, axis=0, keepdims=True), sink))
        p = jnp.concatenate([jnp.exp2(s_prev - m), jnp.exp2(s_mid - m), jnp.exp2(s_next - m)], axis=0)
        return p.astype(BF16), jnp.exp2(sink - m)

    def weighted_values(p, sink_term, jj, g, pr):
        vt = vbuf[g * V_ROWS:(g + 1) * V_ROWS, jj * blk:jj * blk + span]
        ot = jnp.dot(vt, p, preferred_element_type=F32)
        denom = ot[A_HEAD_DIM:A_HEAD_DIM + 1, :] + sink_term
        ot = ot[0:A_HEAD_DIM, :] * (1.0 / denom)
        both = jnp.concatenate([ot[:, 0:blk], ot[:, blk:]], axis=0)
        o_ref[jj * blk:(jj + 1) * blk, pair_lanes(g, pr)] = both.T.astype(BF16)

    s_stage = {}
    p_stage = {}
    for n in range(len(units) + 2):
        if n < len(units):
            s_stage[n] = scores(*units[n])
        if 0 <= n - 1 < len(units):
            p_stage[n - 1] = softmax(s_stage.pop(n - 1), *units[n - 1])
        if 0 <= n - 2 < len(units):
            weighted_values(*p_stage.pop(n - 2), *units[n - 2])


def _window_attention(qt, kd, vt, sink, batch, seq_len):
    q_cols, n = qt.shape
    v_cols = vt.shape[0]
    tq = ATTN_Q_TILE
    blk = WINDOW
    nt = seq_len // tq
    per_tile = tq // blk
    nblk = seq_len // blk
    k_dup = kd.shape[1]

    def main_idx(b, i):
        return b * nt + i

    def prev_idx(b, i):
        return b * nblk + jnp.maximum(i * per_tile - 1, 0)

    def next_idx(b, i):
        return b * nblk + jnp.minimum((i + 1) * per_tile, nblk - 1)

    k_specs = [pl.BlockSpec((blk, k_dup), lambda b, i: (prev_idx(b, i), 0)),
               pl.BlockSpec((tq, k_dup), lambda b, i: (main_idx(b, i), 0)),
               pl.BlockSpec((blk, k_dup), lambda b, i: (next_idx(b, i), 0))]
    v_specs = [pl.BlockSpec((v_cols, blk), lambda b, i: (0, prev_idx(b, i))),
               pl.BlockSpec((v_cols, tq), lambda b, i: (0, main_idx(b, i))),
               pl.BlockSpec((v_cols, blk), lambda b, i: (0, next_idx(b, i)))]
    return pl.pallas_call(
        _window_attn_kernel,
        grid=(batch, nt),
        in_specs=[pl.BlockSpec(memory_space=pltpu.SMEM),
                  pl.BlockSpec((q_cols, tq), lambda b, i: (0, main_idx(b, i)))] + k_specs + v_specs,
        out_specs=pl.BlockSpec((tq, q_cols), lambda b, i: (main_idx(b, i), 0)),
        out_shape=jax.ShapeDtypeStruct((n, q_cols), BF16),
        scratch_shapes=[pltpu.VMEM((tq + 2 * blk, k_dup), BF16),
                        pltpu.VMEM((A_KV_HEADS * V_ROWS, tq + 2 * blk), BF16)],
        compiler_params=_compiler_params(("parallel", "parallel")),
        name="window_attn",
    )(sink, qt, kd, kd, kd, vt, vt, vt)


def _residual_mlp_tail(mixer_out, x_ref, wo_ref, ln1g_ref, ln1b_ref, w1_ref, w2_ref, ln2g_ref, ln2b_ref,
                       out_ref):
    tm = x_ref.shape[0]
    halves = [slice(0, tm // 2), slice(tm // 2, tm)]
    n_ff = D_FF // FF_CHUNK

    def project(rows):
        return jnp.dot(mixer_out(rows), wo_ref[...], preferred_element_type=F32)

    def norm1(y, rows):
        x1 = _layer_norm(DN_ALPHA * x_ref[rows, :] + y, ln1g_ref[...], ln1b_ref[...])
        return x1, x1.astype(BF16)

    def mlp_chunk(x1b, acc, c):
        hc = jnp.dot(x1b, w1_ref[:, c * FF_CHUNK:(c + 1) * FF_CHUNK], preferred_element_type=F32)
        hc = jnp.square(jnp.maximum(hc, 0.0)).astype(BF16)
        part = jnp.dot(hc, w2_ref[c * FF_CHUNK:(c + 1) * FF_CHUNK, :], preferred_element_type=F32)
        return part if acc is None else acc + part

    def norm2(x1, acc, rows):
        out_ref[rows, :] = _layer_norm(DN_ALPHA * x1 + acc, ln2g_ref[...], ln2b_ref[...])

    a, b = halves
    x1_a, x1b_a = norm1(project(a), a)
    y_b = project(b)
    acc_a = mlp_chunk(x1b_a, None, 0)
    x1_b, x1b_b = norm1(y_b, b)
    for c in range(1, n_ff):
        acc_a = mlp_chunk(x1b_a, acc_a, c)
    acc_b = mlp_chunk(x1b_b, None, 0)
    norm2(x1_a, acc_a, a)
    for c in range(1, n_ff):
        acc_b = mlp_chunk(x1b_b, acc_b, c)
    norm2(x1_b, acc_b, b)


def _attn_post_kernel(o_ref, x_ref, wo_ref, ln1g_ref, ln1b_ref, w1_ref, w2_ref, ln2g_ref, ln2b_ref, out_ref):
    _residual_mlp_tail(lambda rows: o_ref[rows, :], x_ref, wo_ref, ln1g_ref, ln1b_ref, w1_ref, w2_ref,
                       ln2g_ref, ln2b_ref, out_ref)


def _gla_post_kernel(of_ref, ob_ref, r_ref, ng_ref, x_ref, wo_ref, ln1g_ref, ln1b_ref, w1_ref, w2_ref,
                     ln2g_ref, ln2b_ref, out_ref):
    def mixer_out(rows):
        o = of_ref[rows, :] + ob_ref[rows, :]
        r = r_ref[rows, :]
        gate = r * (1.0 / (1.0 + jnp.exp(-r)))
        ng = ng_ref[...]
        heads = []
        for h in range(B_HEADS):
            oh = o[:, h * B_VAL_DIM:(h + 1) * B_VAL_DIM]
            oh = oh * lax.rsqrt(jnp.mean(oh * oh, axis=-1, keepdims=True) + HEAD_NORM_EPS)
            heads.append((oh * ng * gate[:, h * B_VAL_DIM:(h + 1) * B_VAL_DIM]).astype(BF16))
        return jnp.concatenate(heads, axis=-1)

    _residual_mlp_tail(mixer_out, x_ref, wo_ref, ln1g_ref, ln1b_ref, w1_ref, w2_ref, ln2g_ref, ln2b_ref,
                       out_ref)


def _post_mixer(kernel_fn, mixer_inputs, extra_consts, x2, wo, ln1g, ln1b, w1, w2, ln2g, ln2b):
    n = x2.shape[0]
    tm = TOKEN_TILE
    tile = lambda cols: pl.BlockSpec((tm, cols), lambda i: (i, 0))
    consts = list(extra_consts) + []
    return pl.pallas_call(
        kernel_fn,
        grid=(n // tm,),
        in_specs=[tile(a.shape[1]) for a in mixer_inputs]
        + [_const_spec(c.shape) for c in consts]
        + [tile(D_MODEL), _const_spec(wo.shape), _const_spec(ln1g.shape), _const_spec(ln1b.shape),
           _const_spec(w1.shape), _const_spec(w2.shape), _const_spec(ln2g.shape), _const_spec(ln2b.shape)],
        out_specs=tile(D_MODEL),
        out_shape=jax.ShapeDtypeStruct((n, D_MODEL), F32),
        compiler_params=_compiler_params(("parallel",)),
        name=kernel_fn.__name__.strip("_"),
    )(*mixer_inputs, *consts, x2, wo, ln1g, ln1b, w1, w2, ln2g, ln2b)


def _gla_in_kernel(x_ref, w_ref, w2f_ref, gbf_ref, w2b_ref, gbb_ref,
                   q_ref, k_ref, v_ref, r_ref, bf_ref, bb_ref, tot_ref):
    xb = x_ref[...].astype(BF16)
    c1 = B_QK_COLS
    c2 = 2 * B_QK_COLS
    c3 = c2 + B_V_COLS
    c4 = c3 + B_V_COLS
    n_chunks = x_ref.shape[0] // B_CHUNK

    def project(lo, hi):
        return jnp.dot(xb, w_ref[:, lo:hi], preferred_element_type=F32)

    def chunk_totals(b, reverse):
        edge = 0 if reverse else B_CHUNK - 1
        return [b[c * B_CHUNK + edge:c * B_CHUNK + edge + 1, :] for c in range(n_chunks)]

    lr = project(c4, c4 + LANES).astype(BF16)
    q_ref[...] = project(0, c1) * (B_KEY_DIM ** -0.5)
    g_f = _log_gate(lr, w2f_ref[...], gbf_ref[...])
    k_ref[...] = project(c1, c2)
    g_b = _log_gate(lr, w2b_ref[...], gbb_ref[...])
    v_ref[...] = project(c2, c3).astype(BF16)
    b_f = _chunk_cumsum(g_f, False)
    bf_ref[...] = b_f
    b_b = _chunk_cumsum(g_b, True)
    bb_ref[...] = b_b
    r_ref[...] = project(c3, c4)
    tot_ref[0] = jnp.concatenate(chunk_totals(b_f, False) + chunk_totals(b_b, True), axis=0)


def _gla_in_proj(x2, w, w2f, gbf, w2b, gbb):
    n = x2.shape[0]
    tm = TOKEN_TILE
    n_tot = 2 * (tm // B_CHUNK)
    tile = lambda cols: pl.BlockSpec((tm, cols), lambda i: (i, 0))
    return pl.pallas_call(
        _gla_in_kernel,
        grid=(n // tm,),
        in_specs=[tile(D_MODEL), _const_spec(w.shape), _const_spec(w2f.shape), _const_spec(gbf.shape),
                  _const_spec(w2b.shape), _const_spec(gbb.shape)],
        out_specs=[tile(B_QK_COLS), tile(B_QK_COLS), tile(B_V_COLS), tile(B_V_COLS), tile(B_QK_COLS),
                   tile(B_QK_COLS), pl.BlockSpec((1, n_tot, B_QK_COLS), lambda i: (i, 0, 0))],
        out_shape=[
            jax.ShapeDtypeStruct((n, B_QK_COLS), F32),
            jax.ShapeDtypeStruct((n, B_QK_COLS), F32),
            jax.ShapeDtypeStruct((n, B_V_COLS), BF16),
            jax.ShapeDtypeStruct((n, B_V_COLS), F32),
            jax.ShapeDtypeStruct((n, B_QK_COLS), F32),
            jax.ShapeDtypeStruct((n, B_QK_COLS), F32),
            jax.ShapeDtypeStruct((n // tm, n_tot, B_QK_COLS), F32),
        ],
        compiler_params=_compiler_params(("parallel",)),
        name="gla_in_proj",
    )(x2, w, w2f, gbf, w2b, gbb)


def _log_gate(lr, w2, bias):
    z = jnp.dot(lr, w2, preferred_element_type=F32) + bias
    log_term = jnp.log2(1.0 + jnp.exp(-jnp.abs(z)))
    return jnp.minimum(z, 0.0) * (1.0 / B_GATE_TAU) - log_term * (LN_2 / B_GATE_TAU)


def _chunk_cumsum(g, reverse):
    hi = g.astype(BF16)
    rest = g - hi.astype(F32)
    mid = rest.astype(BF16)
    lo = (rest - mid.astype(F32)).astype(BF16)
    t_row = lax.broadcasted_iota(jnp.int32, (B_CHUNK, 3 * B_CHUNK), 0)
    t_col = lax.broadcasted_iota(jnp.int32, (B_CHUNK, 3 * B_CHUNK), 1) % B_CHUNK
    keep = (t_row <= t_col) if reverse else (t_row >= t_col)
    tri = jnp.where(keep, 1.0, 0.0).astype(BF16)
    out = []
    for c in range(g.shape[0] // B_CHUNK):
        rows = slice(c * B_CHUNK, (c + 1) * B_CHUNK)
        terms = jnp.concatenate([hi[rows], mid[rows], lo[rows]], axis=0)
        out.append(jnp.dot(tri, terms, preferred_element_type=F32))
    return jnp.concatenate(out, axis=0)


def _gla_scan_kernel(qf_ref, kf_ref, vf_ref, bf_ref, totf_ref, qb_ref, kb_ref, vb_ref, bb_ref, totb_ref,
                     of_ref, ob_ref, d_ref, sf_ref, sb_ref):
    @pl.when(pl.program_id(1) == 0)
    def _():
        sf_ref[...] = jnp.zeros_like(sf_ref)
        sb_ref[...] = jnp.zeros_like(sb_ref)

    n_chunks = GLA_TILE // B_CHUNK
    pair_k = 2 * B_KEY_DIM
    totals = jnp.concatenate([totf_ref[0, 0:n_chunks, :], totb_ref[0, n_chunks:, :]], axis=0)
    d_ref[...] = jnp.exp(totals).T

    directions = ((False, qf_ref, kf_ref, vf_ref, bf_ref, 0, of_ref, sf_ref),
                  (True, qb_ref, kb_ref, vb_ref, bb_ref, n_chunks, ob_ref, sb_ref))
    t_row = lax.broadcasted_iota(jnp.int32, (B_CHUNK, 2 * B_CHUNK), 0)
    t_col = _lane_index((B_CHUNK, 2 * B_CHUNK)) % B_CHUNK
    first_head_lanes = _lane_index((B_CHUNK, pair_k)) < B_KEY_DIM
    zero_k = jnp.zeros((B_CHUNK, pair_k), BF16)
    zero_v = jnp.zeros((B_CHUNK, B_VAL_DIM), BF16)
    units = [(c, d, hp) for c in range(n_chunks) for d in range(2) for hp in range(B_HEADS // 2)]

    def chunk_rows(c, reverse):
        cc = n_chunks - 1 - c if reverse else c
        return cc, slice(cc * B_CHUNK, (cc + 1) * B_CHUNK)

    def prepare(c, d, hp):
        reverse, q_ref, k_ref, v_ref, b_ref = directions[d][:5]
        _, rows = chunk_rows(c, reverse)
        cols = slice(hp * pair_k, (hp + 1) * pair_k)
        b = b_ref[rows, cols]
        edge = 0 if reverse else B_CHUNK - 1
        b_edge = b[edge:edge + 1, :]
        k = k_ref[rows, cols]
        qe = (q_ref[rows, cols] * jnp.exp(b)).astype(BF16)
        ke = (k * jnp.exp(-b)).astype(BF16)
        kd = (k * jnp.exp(b_edge - b)).astype(BF16)
        ke_bd = jnp.concatenate([jnp.where(first_head_lanes, ke, zero_k),
                                 jnp.where(first_head_lanes, zero_k, ke)], axis=0)
        att = lax.dot_general(qe, ke_bd, (((1,), (1,)), ((), ())), preferred_element_type=F32)
        causal = (t_row <= t_col) if reverse else (t_row >= t_col)
        att = jnp.where(causal, att, 0.0).astype(BF16)
        updates = []
        for hh in range(2):
            h = 2 * hp + hh
            vh = v_ref[rows, h * B_VAL_DIM:(h + 1) * B_VAL_DIM]
            updates.append(lax.dot_general(kd[:, hh * B_KEY_DIM:(hh + 1) * B_KEY_DIM], vh,
                                           (((0,), (0,)), ((), ())), preferred_element_type=F32))
        return qe, att, updates

    def apply(prepared, c, d, hp):
        reverse, _, _, v_ref, _, d_col0, o_ref, s_ref = directions[d]
        qe, att, updates = prepared
        cc, rows = chunk_rows(c, reverse)
        for hh in range(2):
            h = 2 * hp + hh
            vs = slice(h * B_VAL_DIM, (h + 1) * B_VAL_DIM)
            vh = v_ref[rows, vs]
            state = s_ref[h]
            lhs = jnp.concatenate([qe[:, hh * B_KEY_DIM:(hh + 1) * B_KEY_DIM], att], axis=1)
            rhs = jnp.concatenate([state.astype(BF16)] + ([vh, zero_v] if hh == 0 else [zero_v, vh]), axis=0)
            o_ref[rows, vs] = jnp.dot(lhs, rhs, preferred_element_type=F32)
            decay = d_ref[h * B_KEY_DIM:(h + 1) * B_KEY_DIM, d_col0 + cc:d_col0 + cc + 1]
            s_ref[h] = state * decay + updates[hh]

    prepared = {}
    for n in range(len(units) + 1):
        if n < len(units):
            prepared[n] = prepare(*units[n])
        if n >= 1:
            apply(prepared.pop(n - 1), *units[n - 1])


def _gla_scan(q, k, v, b_f, b_b, tot, batch, seq_len):
    n = q.shape[0]
    ts = GLA_TILE
    nt = seq_len // ts
    n_tot = tot.shape[1]

    def fwd_tile(b, i):
        return b * nt + i

    def bwd_tile(b, i):
        return b * nt + nt - 1 - i

    def token_specs(tile_of):
        rows = lambda b, i: (tile_of(b, i), 0)
        return [pl.BlockSpec((ts, B_QK_COLS), rows), pl.BlockSpec((ts, B_QK_COLS), rows),
                pl.BlockSpec((ts, B_V_COLS), rows), pl.BlockSpec((ts, B_QK_COLS), rows),
                pl.BlockSpec((1, n_tot, B_QK_COLS), lambda b, i: (tile_of(b, i), 0, 0))]

    return pl.pallas_call(
        _gla_scan_kernel,
        grid=(batch, nt),
        in_specs=token_specs(fwd_tile) + token_specs(bwd_tile),
        out_specs=[pl.BlockSpec((ts, B_V_COLS), lambda b, i: (fwd_tile(b, i), 0)),
                   pl.BlockSpec((ts, B_V_COLS), lambda b, i: (bwd_tile(b, i), 0))],
        out_shape=[jax.ShapeDtypeStruct((n, B_V_COLS), F32), jax.ShapeDtypeStruct((n, B_V_COLS), F32)],
        scratch_shapes=[
            pltpu.VMEM((B_QK_COLS, n_tot), F32),
            pltpu.VMEM((B_HEADS, B_KEY_DIM, B_VAL_DIM), F32),
            pltpu.VMEM((B_HEADS, B_KEY_DIM, B_VAL_DIM), F32),
        ],
        compiler_params=_compiler_params(("parallel", "arbitrary")),
        name="gla_scan",
    )(q, k, v, b_f, tot, q, k, v, b_b, tot)


def _row(v):
    return v.reshape(1, -1).astype(F32)


def _pad_gate_weight(w2, first_row):
    out = jnp.zeros((LANES, w2.shape[1]), BF16)
    return lax.dynamic_update_slice(out, w2.astype(BF16), (first_row, 0))


def kernel(x, positions, attn_w_in, attn_sink, attn_w_out, gla_w_in, gla_gate_w2_fwd, gla_gate_b_fwd,
           gla_gate_w2_bwd, gla_gate_b_bwd, gla_norm_g, gla_w_out, mix_ln_g, mix_ln_b, mlp_w1, mlp_w2,
           mlp_ln_g, mlp_ln_b):
    batch, seq_len, d_model = x.shape
    n = batch * seq_len
    x2 = x.reshape(n, d_model)

    inv_freq = ROPE_THETA ** (-jnp.arange(0, A_HEAD_DIM, 2, dtype=F32) / A_HEAD_DIM)
    ang_t = inv_freq[:, None] * positions.astype(F32).reshape(1, n)
    q_cols = A_HEADS * A_HEAD_DIM
    k_cols = A_KV_HEADS * A_HEAD_DIM

    for i in range(DEPTH):
        j = i // 2
        w1 = mlp_w1[i].astype(BF16)
        w2 = mlp_w2[i].astype(BF16)
        tail = (_row(mix_ln_g[i]), _row(mix_ln_b[i]), w1, w2, _row(mlp_ln_g[i]), _row(mlp_ln_b[i]))
        if i % 2 == 0:
            w_in = attn_w_in[j].astype(BF16)
            wqv_t = jnp.concatenate([w_in[:, :q_cols], w_in[:, q_cols + k_cols:]], axis=1).T
            qt, kd, vt = _attn_in_proj(x2, ang_t, wqv_t, w_in[:, q_cols:q_cols + k_cols])
            o = _window_attention(qt, kd, vt, attn_sink[j].astype(F32), batch, seq_len)
            x2 = _post_mixer(_attn_post_kernel, [o], [], x2, attn_w_out[j].astype(BF16), *tail)
        else:
            w_in = gla_w_in[j]
            lr_cols = 2 * B_GATE_RANK
            w_in = jnp.pad(w_in, ((0, 0), (0, LANES - lr_cols))).astype(BF16)
            q, k, v, r, b_f, b_b, tot = _gla_in_proj(
                x2, w_in, _pad_gate_weight(gla_gate_w2_fwd[j], 0), _row(gla_gate_b_fwd[j]),
                _pad_gate_weight(gla_gate_w2_bwd[j], B_GATE_RANK), _row(gla_gate_b_bwd[j]))
            o_f, o_b = _gla_scan(q, k, v, b_f, b_b, tot, batch, seq_len)
            x2 = _post_mixer(_gla_post_kernel, [o_f, o_b, r], [_row(gla_norm_g[j])], x2,
                             gla_w_out[j].astype(BF16), *tail)
    return x2.reshape(batch, seq_len, d_model)
```

```python
import functools

import jax
import jax.numpy as jnp
from jax import lax
from jax.experimental import pallas as pl
from jax.experimental.pallas import tpu as pltpu

D_MODEL = 1024
DEPTH = 2

A_HEADS = 16
A_KV_HEADS = 4
A_HEAD_DIM = D_MODEL // A_HEADS
A_GROUP = A_HEADS // A_KV_HEADS
WINDOW = 128
ROPE_THETA = 10000.0

B_HEADS = 4
B_KEY_DIM = (D_MODEL // 2) // B_HEADS
B_VAL_DIM = D_MODEL // B_HEADS
B_GATE_RANK = 16
B_GATE_TAU = 16.0
B_CHUNK = 64
B_QK_COLS = B_HEADS * B_KEY_DIM
B_V_COLS = B_HEADS * B_VAL_DIM

D_FF = 4 * D_MODEL
DN_ALPHA = float((2 * DEPTH) ** 0.25)
LN_EPS = 1e-5
HEAD_NORM_EPS = 1e-6
LN_2 = 0.6931471805599453
LOG2_E = 1.4426950408889634

LANES = 128
BF16_ROWS = 16
V_ROWS = A_HEAD_DIM + BF16_ROWS
VMEM_LIMIT_BYTES = 56 * 1024 * 1024

TOKEN_TILE = 512
ATTN_Q_TILE = 1024
GLA_TILE = TOKEN_TILE
FF_CHUNK = 1024

BF16 = jnp.bfloat16
F32 = jnp.float32


def _const_spec(shape):
    zeros = (0,) * len(shape)
    return pl.BlockSpec(shape, lambda *_: zeros, pipeline_mode=pl.Buffered(1))


def _compiler_params(semantics, flags=None):
    return pltpu.CompilerParams(dimension_semantics=semantics, vmem_limit_bytes=VMEM_LIMIT_BYTES, flags=flags)


def _layer_norm(z, g, b):
    mu = jnp.mean(z, axis=-1, keepdims=True)
    zc = z - mu
    var = jnp.mean(zc * zc, axis=-1, keepdims=True)
    return zc * lax.rsqrt(var + LN_EPS) * g + b


def _lane_index(shape):
    return lax.broadcasted_iota(jnp.int32, shape, len(shape) - 1)


def _attn_in_kernel(x_ref, ang_ref, wqv_ref, wk_ref, qt_ref, k_ref, vt_ref):
    xb = x_ref[...].astype(BF16)
    half = A_HEAD_DIM // 2
    q_cols = A_HEADS * A_HEAD_DIM
    kv_cols = A_KV_HEADS * A_HEAD_DIM
    cos_t = jnp.cos(ang_ref[...])
    sin_t = jnp.sin(ang_ref[...])

    hqv = lax.dot_general(wqv_ref[...], xb, (((1,), (1,)), ((), ())), preferred_element_type=F32)
    scale = A_HEAD_DIM ** -0.5 * LOG2_E
    for h in range(A_HEADS):
        t1 = hqv[h * A_HEAD_DIM:h * A_HEAD_DIM + half, :]
        t2 = hqv[h * A_HEAD_DIM + half:(h + 1) * A_HEAD_DIM, :]
        qt_ref[h * A_HEAD_DIM:h * A_HEAD_DIM + half, :] = ((t1 * cos_t - t2 * sin_t) * scale).astype(BF16)
        qt_ref[h * A_HEAD_DIM + half:(h + 1) * A_HEAD_DIM, :] = ((t2 * cos_t + t1 * sin_t) * scale).astype(BF16)
    vt_ref[...] = hqv[q_cols:, :].astype(BF16)

    cos = jnp.concatenate([cos_t] * (LANES // half), axis=0).T
    sin = jnp.concatenate([sin_t] * (LANES // half), axis=0).T
    lane = _lane_index(cos.shape)
    first_half = (lane % A_HEAD_DIM) < half
    sin_signed = jnp.where(first_half, -sin, sin)
    low_head = lane < A_HEAD_DIM
    hk = jnp.dot(xb, wk_ref[...], preferred_element_type=F32)
    for j in range(kv_cols // LANES):
        t = hk[:, j * LANES:(j + 1) * LANES]
        partner = jnp.where(first_half, pltpu.roll(t, LANES - half, 1), pltpu.roll(t, half, 1))
        t = t * cos + partner * sin_signed
        swapped = pltpu.roll(t, A_HEAD_DIM, 1)
        k_ref[:, (2 * j) * LANES:(2 * j + 1) * LANES] = jnp.where(low_head, t, swapped).astype(BF16)
        k_ref[:, (2 * j + 1) * LANES:(2 * j + 2) * LANES] = jnp.where(low_head, swapped, t).astype(BF16)


def _attn_in_proj(x2, ang_t, wqv_t, wk):
    n = x2.shape[0]
    tm = TOKEN_TILE
    q_cols = A_HEADS * A_HEAD_DIM
    kv_cols = A_KV_HEADS * A_HEAD_DIM
    k_dup = A_KV_HEADS * LANES
    return pl.pallas_call(
        _attn_in_kernel,
        grid=(n // tm,),
        in_specs=[
            pl.BlockSpec((tm, D_MODEL), lambda i: (i, 0)),
            pl.BlockSpec((ang_t.shape[0], tm), lambda i: (0, i)),
            _const_spec(wqv_t.shape),
            _const_spec(wk.shape),
        ],
        out_specs=[
            pl.BlockSpec((q_cols, tm), lambda i: (0, i)),
            pl.BlockSpec((tm, k_dup), lambda i: (i, 0)),
            pl.BlockSpec((kv_cols, tm), lambda i: (0, i)),
        ],
        out_shape=[
            jax.ShapeDtypeStruct((q_cols, n), BF16),
            jax.ShapeDtypeStruct((n, k_dup), BF16),
            jax.ShapeDtypeStruct((kv_cols, n), BF16),
        ],
        compiler_params=_compiler_params(("parallel",)),
        name="attn_in_proj",
    )(x2, ang_t, wqv_t, wk)


def _window_attn_kernel(sink_ref, qt_ref, kp_ref, km_ref, kn_ref, vp_ref, vm_ref, vn_ref, o_ref,
                        kbuf, vbuf):
    tq = ATTN_Q_TILE
    blk = WINDOW
    span = 3 * blk
    i = pl.program_id(1)
    last_tile = pl.num_programs(1) - 1
    kbuf[0:blk, :] = kp_ref[...]
    kbuf[blk:blk + tq, :] = km_ref[...]
    kbuf[blk + tq:, :] = kn_ref[...]
    for g in range(A_KV_HEADS):
        src = slice(g * A_HEAD_DIM, (g + 1) * A_HEAD_DIM)
        dst = slice(g * V_ROWS, g * V_ROWS + A_HEAD_DIM)
        vbuf[dst, 0:blk] = vp_ref[src, :]
        vbuf[dst, blk:blk + tq] = vm_ref[src, :]
        vbuf[dst, blk + tq:] = vn_ref[src, :]
        vbuf[g * V_ROWS + A_HEAD_DIM:(g + 1) * V_ROWS, :] = jnp.ones((BF16_ROWS, tq + 2 * blk), BF16)

    pair_shape = (blk, 2 * LANES)
    key = lax.broadcasted_iota(jnp.int32, pair_shape, 0)
    query = _lane_index(pair_shape) % blk
    neg_inf = jnp.float32(-jnp.inf)
    prev_bias = jnp.where(key >= query, 0.0, neg_inf)
    next_bias = jnp.where(key <= query, 0.0, neg_inf)
    first_prev_bias = jnp.where(i == 0, neg_inf, prev_bias)
    last_next_bias = jnp.where(i == last_tile, neg_inf, next_bias)
    low_rows = lax.broadcasted_iota(jnp.int32, (LANES, blk), 0) < A_HEAD_DIM
    zero = jnp.zeros((LANES, blk), BF16)

    n_blocks = tq // blk
    units = [(jj, g, pr) for jj in range(n_blocks) for g in range(A_KV_HEADS) for pr in range(A_GROUP // 2)]

    def pair_lanes(g, pr):
        return slice((g * 2 + pr) * LANES, (g * 2 + pr + 1) * LANES)

    def scores(jj, g, pr):
        kk = kbuf[jj * blk:jj * blk + span, g * LANES:(g + 1) * LANES]
        qt = qt_ref[pair_lanes(g, pr), jj * blk:(jj + 1) * blk]
        w = jnp.concatenate([jnp.where(low_rows, qt, zero), jnp.where(low_rows, zero, qt)], axis=1)
        return jnp.dot(kk, w, preferred_element_type=F32)

    def softmax(s, jj, g, pr):
        s_prev = s[0:blk] + (first_prev_bias if jj == 0 else prev_bias)
        s_mid = s[blk:2 * blk]
        s_next = s[2 * blk:] + (last_next_bias if jj == n_blocks - 1 else next_bias)
        head = g * A_GROUP + 2 * pr
        sink = jnp.concatenate([jnp.full((1, blk), sink_ref[head] * LOG2_E, F32),
                                jnp.full((1, blk), sink_ref[head + 1] * LOG2_E, F32)], axis=1)
        m = jnp.maximum(jnp.maximum(jnp.max(s_prev, axis=0, keepdims=True),
                                    jnp.max(s_mid, axis=0, keepdims=True)),
                        jnp.maximum(jnp.max(s_next, axis=0, keepdims=True), sink))
        p = jnp.concatenate([jnp.exp2(s_prev - m), jnp.exp2(s_mid - m), jnp.exp2(s_next - m)], axis=0)
        return p.astype(BF16), jnp.exp2(sink - m)

    def weighted_values(p, sink_term, jj, g, pr):
        vt = vbuf[g * V_ROWS:(g + 1) * V_ROWS, jj * blk:jj * blk + span]
        ot = jnp.dot(vt, p, preferred_element_type=F32)
        denom = ot[A_HEAD_DIM:A_HEAD_DIM + 1, :] + sink_term
        ot = ot[0:A_HEAD_DIM, :] * (1.0 / denom)
        both = jnp.concatenate([ot[:, 0:blk], ot[:, blk:]], axis=0)
        o_ref[jj * blk:(jj + 1) * blk, pair_lanes(g, pr)] = both.T.astype(BF16)

    s_stage = {}
    p_stage = {}
    for n in range(len(units) + 2):
        if n < len(units):
            s_stage[n] = scores(*units[n])
        if 0 <= n - 1 < len(units):
            p_stage[n - 1] = softmax(s_stage.pop(n - 1), *units[n - 1])
        if 0 <= n - 2 < len(units):
            weighted_values(*p_stage.pop(n - 2), *units[n - 2])


def _window_attention(qt, kd, vt, sink, batch, seq_len):
    q_cols, n = qt.shape
    v_cols = vt.shape[0]
    tq = ATTN_Q_TILE
    blk = WINDOW
    nt = seq_len // tq
    per_tile = tq // blk
    nblk = seq_len // blk
    k_dup = kd.shape[1]

    def main_idx(b, i):
        return b * nt + i

    def prev_idx(b, i):
        return b * nblk + jnp.maximum(i * per_tile - 1, 0)

    def next_idx(b, i):
        return b * nblk + jnp.minimum((i + 1) * per_tile, nblk - 1)

    k_specs = [pl.BlockSpec((blk, k_dup), lambda b, i: (prev_idx(b, i), 0)),
               pl.BlockSpec((tq, k_dup), lambda b, i: (main_idx(b, i), 0)),
               pl.BlockSpec((blk, k_dup), lambda b, i: (next_idx(b, i), 0))]
    v_specs = [pl.BlockSpec((v_cols, blk), lambda b, i: (0, prev_idx(b, i))),
               pl.BlockSpec((v_cols, tq), lambda b, i: (0, main_idx(b, i))),
               pl.BlockSpec((v_cols, blk), lambda b, i: (0, next_idx(b, i)))]
    return pl.pallas_call(
        _window_attn_kernel,
        grid=(batch, nt),
        in_specs=[pl.BlockSpec(memory_space=pltpu.SMEM),
                  pl.BlockSpec((q_cols, tq), lambda b, i: (0, main_idx(b, i)))] + k_specs + v_specs,
        out_specs=pl.BlockSpec((tq, q_cols), lambda b, i: (main_idx(b, i), 0)),
        out_shape=jax.ShapeDtypeStruct((n, q_cols), BF16),
        scratch_shapes=[pltpu.VMEM((tq + 2 * blk, k_dup), BF16),
                        pltpu.VMEM((A_KV_HEADS * V_ROWS, tq + 2 * blk), BF16)],
        compiler_params=_compiler_params(("parallel", "parallel")),
        name="window_attn",
    )(sink, qt, kd, kd, kd, vt, vt, vt)


def _residual_mlp_tail(mixer_out, x_ref, wo_ref, ln1g_ref, ln1b_ref, w1_ref, w2_ref, ln2g_ref, ln2b_ref,
                       out_ref):
    tm = x_ref.shape[0]
    halves = [slice(0, tm // 2), slice(tm // 2, tm)]
    n_ff = D_FF // FF_CHUNK

    def project(rows):
        return jnp.dot(mixer_out(rows), wo_ref[...], preferred_element_type=F32)

    def norm1(y, rows):
        x1 = _layer_norm(DN_ALPHA * x_ref[rows, :] + y, ln1g_ref[...], ln1b_ref[...])
        return x1, x1.astype(BF16)

    def mlp_chunk(x1b, acc, c):
        hc = jnp.dot(x1b, w1_ref[:, c * FF_CHUNK:(c + 1) * FF_CHUNK], preferred_element_type=F32)
        hc = jnp.square(jnp.maximum(hc, 0.0)).astype(BF16)
        part = jnp.dot(hc, w2_ref[c * FF_CHUNK:(c + 1) * FF_CHUNK, :], preferred_element_type=F32)
        return part if acc is None else acc + part

    def norm2(x1, acc, rows):
        out_ref[rows, :] = _layer_norm(DN_ALPHA * x1 + acc, ln2g_ref[...], ln2b_ref[...])

    a, b = halves
    x1_a, x1b_a = norm1(project(a), a)
    y_b = project(b)
    acc_a = mlp_chunk(x1b_a, None, 0)
    x1_b, x1b_b = norm1(y_b, b)
    for c in range(1, n_ff):
        acc_a = mlp_chunk(x1b_a, acc_a, c)
    acc_b = mlp_chunk(x1b_b, None, 0)
    norm2(x1_a, acc_a, a)
    for c in range(1, n_ff):
        acc_b = mlp_chunk(x1b_b, acc_b, c)
    norm2(x1_b, acc_b, b)


def _attn_post_kernel(o_ref, x_ref, wo_ref, ln1g_ref, ln1b_ref, w1_ref, w2_ref, ln2g_ref, ln2b_ref, out_ref):
    _residual_mlp_tail(lambda rows: o_ref[rows, :], x_ref, wo_ref, ln1g_ref, ln1b_ref, w1_ref, w2_ref,
                       ln2g_ref, ln2b_ref, out_ref)


def _gla_post_kernel(of_ref, ob_ref, r_ref, ng_ref, x_ref, wo_ref, ln1g_ref, ln1b_ref, w1_ref, w2_ref,
                     ln2g_ref, ln2b_ref, out_ref):
    def mixer_out(rows):
        o = of_ref[rows, :] + ob_ref[rows, :]
        r = r_ref[rows, :]
        gate = r * (1.0 / (1.0 + jnp.exp(-r)))
        ng = ng_ref[...]
        heads = []
        for h in range(B_HEADS):
            oh = o[:, h * B_VAL_DIM:(h + 1) * B_VAL_DIM]
            oh = oh * lax.rsqrt(jnp.mean(oh * oh, axis=-1, keepdims=True) + HEAD_NORM_EPS)
            heads.append((oh * ng * gate[:, h * B_VAL_DIM:(h + 1) * B_VAL_DIM]).astype(BF16))
        return jnp.concatenate(heads, axis=-1)

    _residual_mlp_tail(mixer_out, x_ref, wo_ref, ln1g_ref, ln1b_ref, w1_ref, w2_ref, ln2g_ref, ln2b_ref,
                       out_ref)


def _post_mixer(kernel_fn, mixer_inputs, extra_consts, x2, wo, ln1g, ln1b, w1, w2, ln2g, ln2b):
    n = x2.shape[0]
    tm = TOKEN_TILE
    tile = lambda cols: pl.BlockSpec((tm, cols), lambda i: (i, 0))
    consts = list(extra_consts) + []
    return pl.pallas_call(
        kernel_fn,
        grid=(n // tm,),
        in_specs=[tile(a.shape[1]) for a in mixer_inputs]
        + [_const_spec(c.shape) for c in consts]
        + [tile(D_MODEL), _const_spec(wo.shape), _const_spec(ln1g.shape), _const_spec(ln1b.shape),
           _const_spec(w1.shape), _const_spec(w2.shape), _const_spec(ln2g.shape), _const_spec(ln2b.shape)],
        out_specs=tile(D_MODEL),
        out_shape=jax.ShapeDtypeStruct((n, D_MODEL), F32),
        compiler_params=_compiler_params(("parallel",)),
        name=kernel_fn.__name__.strip("_"),
    )(*mixer_inputs, *consts, x2, wo, ln1g, ln1b, w1, w2, ln2g, ln2b)


def _gla_in_kernel(x_ref, w_ref, w2f_ref, gbf_ref, w2b_ref, gbb_ref,
                   v_ref, r_ref, qef_ref, kef_ref, kdf_ref, qeb_ref, keb_ref, kdb_ref, tot_ref):
    xb = x_ref[...].astype(BF16)
    tm = x_ref.shape[0]
    c1 = B_QK_COLS
    c2 = 2 * B_QK_COLS
    c3 = c2 + B_V_COLS
    c4 = c3 + B_V_COLS
    n_chunks = tm // B_CHUNK

    def project(lo, hi):
        return jnp.dot(xb, w_ref[:, lo:hi], preferred_element_type=F32)

    def chunk_totals(b, reverse):
        edge = 0 if reverse else B_CHUNK - 1
        return [b[c * B_CHUNK + edge:c * B_CHUNK + edge + 1, :] for c in range(n_chunks)]

    def decayed(q, k, b, totals, qe_ref, ke_ref, kd_ref):
        b_total = jnp.concatenate([jnp.broadcast_to(t, (B_CHUNK, t.shape[1])) for t in totals], axis=0)
        qe_ref[...] = (q * jnp.exp(b)).astype(BF16)
        ke_ref[...] = (k * jnp.exp(-b)).astype(BF16)
        kd_ref[...] = (k * jnp.exp(b_total - b)).astype(BF16)

    lr = project(c4, c4 + LANES).astype(BF16)
    q = project(0, c1) * (B_KEY_DIM ** -0.5)
    g_f = _log_gate(lr, w2f_ref[...], gbf_ref[...])
    k = project(c1, c2)
    g_b = _log_gate(lr, w2b_ref[...], gbb_ref[...])
    v_ref[...] = project(c2, c3).astype(BF16)
    b_f = _chunk_cumsum(g_f, False)
    totals_f = chunk_totals(b_f, False)
    decayed(q, k, b_f, totals_f, qef_ref, kef_ref, kdf_ref)
    b_b = _chunk_cumsum(g_b, True)
    r_ref[...] = project(c3, c4)
    totals_b = chunk_totals(b_b, True)
    decayed(q, k, b_b, totals_b, qeb_ref, keb_ref, kdb_ref)
    tot_ref[0] = jnp.concatenate(totals_f + totals_b, axis=0)


def _gla_in_proj(x2, w, w2f, gbf, w2b, gbb):
    n = x2.shape[0]
    tm = TOKEN_TILE
    n_tot = 2 * (tm // B_CHUNK)
    tile = lambda cols: pl.BlockSpec((tm, cols), lambda i: (i, 0))
    qk = jax.ShapeDtypeStruct((n, B_QK_COLS), BF16)
    return pl.pallas_call(
        _gla_in_kernel,
        grid=(n // tm,),
        in_specs=[tile(D_MODEL), _const_spec(w.shape), _const_spec(w2f.shape), _const_spec(gbf.shape),
                  _const_spec(w2b.shape), _const_spec(gbb.shape)],
        out_specs=[tile(B_V_COLS), tile(B_V_COLS)] + [tile(B_QK_COLS)] * 6
        + [pl.BlockSpec((1, n_tot, B_QK_COLS), lambda i: (i, 0, 0))],
        out_shape=[jax.ShapeDtypeStruct((n, B_V_COLS), BF16), jax.ShapeDtypeStruct((n, B_V_COLS), F32)]
        + [qk] * 6 + [jax.ShapeDtypeStruct((n // tm, n_tot, B_QK_COLS), F32)],
        compiler_params=_compiler_params(("parallel",)),
        name="gla_in_proj",
    )(x2, w, w2f, gbf, w2b, gbb)


def _log_gate(lr, w2, bias):
    z = jnp.dot(lr, w2, preferred_element_type=F32) + bias
    log_term = jnp.log2(1.0 + jnp.exp(-jnp.abs(z)))
    return jnp.minimum(z, 0.0) * (1.0 / B_GATE_TAU) - log_term * (LN_2 / B_GATE_TAU)


def _chunk_cumsum(g, reverse):
    hi = g.astype(BF16)
    rest = g - hi.astype(F32)
    mid = rest.astype(BF16)
    lo = (rest - mid.astype(F32)).astype(BF16)
    t_row = lax.broadcasted_iota(jnp.int32, (B_CHUNK, 3 * B_CHUNK), 0)
    t_col = lax.broadcasted_iota(jnp.int32, (B_CHUNK, 3 * B_CHUNK), 1) % B_CHUNK
    keep = (t_row <= t_col) if reverse else (t_row >= t_col)
    tri = jnp.where(keep, 1.0, 0.0).astype(BF16)
    out = []
    for c in range(g.shape[0] // B_CHUNK):
        rows = slice(c * B_CHUNK, (c + 1) * B_CHUNK)
        terms = jnp.concatenate([hi[rows], mid[rows], lo[rows]], axis=0)
        out.append(jnp.dot(tri, terms, preferred_element_type=F32))
    return jnp.concatenate(out, axis=0)


def _gla_scan_kernel(qef_ref, kef_ref, kdf_ref, vf_ref, totf_ref, qeb_ref, keb_ref, kdb_ref, vb_ref, totb_ref,
                     of_ref, ob_ref, d_ref, sf_ref, sb_ref):
    @pl.when(pl.program_id(1) == 0)
    def _():
        sf_ref[...] = jnp.zeros_like(sf_ref)
        sb_ref[...] = jnp.zeros_like(sb_ref)

    n_chunks = GLA_TILE // B_CHUNK
    pair_k = 2 * B_KEY_DIM
    totals = jnp.concatenate([totf_ref[0, 0:n_chunks, :], totb_ref[0, n_chunks:, :]], axis=0)
    d_ref[...] = jnp.exp(totals).T

    directions = ((False, qef_ref, kef_ref, kdf_ref, vf_ref, 0, of_ref, sf_ref),
                  (True, qeb_ref, keb_ref, kdb_ref, vb_ref, n_chunks, ob_ref, sb_ref))
    t_row = lax.broadcasted_iota(jnp.int32, (B_CHUNK, 2 * B_CHUNK), 0)
    t_col = _lane_index((B_CHUNK, 2 * B_CHUNK)) % B_CHUNK
    first_head_lanes = _lane_index((B_CHUNK, pair_k)) < B_KEY_DIM
    zero_k = jnp.zeros((B_CHUNK, pair_k), BF16)
    zero_v = jnp.zeros((B_CHUNK, B_VAL_DIM), BF16)
    units = [(c, d, hp) for c in range(n_chunks) for d in range(2) for hp in range(B_HEADS // 2)]

    def chunk_rows(c, reverse):
        cc = n_chunks - 1 - c if reverse else c
        return cc, slice(cc * B_CHUNK, (cc + 1) * B_CHUNK)

    def prepare(c, d, hp):
        reverse, qe_ref, ke_ref, kd_ref, v_ref = directions[d][:5]
        _, rows = chunk_rows(c, reverse)
        cols = slice(hp * pair_k, (hp + 1) * pair_k)
        qe = qe_ref[rows, cols]
        ke = ke_ref[rows, cols]
        kd = kd_ref[rows, cols]
        ke_bd = jnp.concatenate([jnp.where(first_head_lanes, ke, zero_k),
                                 jnp.where(first_head_lanes, zero_k, ke)], axis=0)
        att = lax.dot_general(qe, ke_bd, (((1,), (1,)), ((), ())), preferred_element_type=F32)
        causal = (t_row <= t_col) if reverse else (t_row >= t_col)
        att = jnp.where(causal, att, 0.0).astype(BF16)
        updates = []
        for hh in range(2):
            h = 2 * hp + hh
            vh = v_ref[rows, h * B_VAL_DIM:(h + 1) * B_VAL_DIM]
            updates.append(lax.dot_general(kd[:, hh * B_KEY_DIM:(hh + 1) * B_KEY_DIM], vh,
                                           (((0,), (0,)), ((), ())), preferred_element_type=F32))
        return qe, att, updates

    def apply(prepared, c, d, hp):
        reverse, _, _, _, v_ref, d_col0, o_ref, s_ref = directions[d]
        qe, att, updates = prepared
        cc, rows = chunk_rows(c, reverse)
        for hh in range(2):
            h = 2 * hp + hh
            vs = slice(h * B_VAL_DIM, (h + 1) * B_VAL_DIM)
            vh = v_ref[rows, vs]
            state = s_ref[h]
            lhs = jnp.concatenate([qe[:, hh * B_KEY_DIM:(hh + 1) * B_KEY_DIM], att], axis=1)
            rhs = jnp.concatenate([state.astype(BF16)] + ([vh, zero_v] if hh == 0 else [zero_v, vh]), axis=0)
            o_ref[rows, vs] = jnp.dot(lhs, rhs, preferred_element_type=F32)
            decay = d_ref[h * B_KEY_DIM:(h + 1) * B_KEY_DIM, d_col0 + cc:d_col0 + cc + 1]
            s_ref[h] = state * decay + updates[hh]

    prepared = {}
    for n in range(len(units) + 1):
        if n < len(units):
            prepared[n] = prepare(*units[n])
        if n >= 1:
            apply(prepared.pop(n - 1), *units[n - 1])


def _gla_scan(v, decayed_f, decayed_b, tot, batch, seq_len):
    n = v.shape[0]
    ts = GLA_TILE
    nt = seq_len // ts
    n_tot = tot.shape[1]

    def fwd_tile(b, i):
        return b * nt + i

    def bwd_tile(b, i):
        return b * nt + nt - 1 - i

    def token_specs(tile_of):
        rows = lambda b, i: (tile_of(b, i), 0)
        return [pl.BlockSpec((ts, B_QK_COLS), rows)] * 3 + [
            pl.BlockSpec((ts, B_V_COLS), rows),
            pl.BlockSpec((1, n_tot, B_QK_COLS), lambda b, i: (tile_of(b, i), 0, 0))]

    return pl.pallas_call(
        _gla_scan_kernel,
        grid=(batch, nt),
        in_specs=token_specs(fwd_tile) + token_specs(bwd_tile),
        out_specs=[pl.BlockSpec((ts, B_V_COLS), lambda b, i: (fwd_tile(b, i), 0)),
                   pl.BlockSpec((ts, B_V_COLS), lambda b, i: (bwd_tile(b, i), 0))],
        out_shape=[jax.ShapeDtypeStruct((n, B_V_COLS), F32), jax.ShapeDtypeStruct((n, B_V_COLS), F32)],
        scratch_shapes=[
            pltpu.VMEM((B_QK_COLS, n_tot), F32),
            pltpu.VMEM((B_HEADS, B_KEY_DIM, B_VAL_DIM), F32),
            pltpu.VMEM((B_HEADS, B_KEY_DIM, B_VAL_DIM), F32),
        ],
        compiler_params=_compiler_params(("parallel", "arbitrary")),
        name="gla_scan",
    )(*decayed_f, v, tot, *decayed_b, v, tot)


def _row(v):
    return v.reshape(1, -1).astype(F32)


def _pad_gate_weight(w2, first_row):
    out = jnp.zeros((LANES, w2.shape[1]), BF16)
    return lax.dynamic_update_slice(out, w2.astype(BF16), (first_row, 0))


def kernel(x, positions, attn_w_in, attn_sink, attn_w_out, gla_w_in, gla_gate_w2_fwd, gla_gate_b_fwd,
           gla_gate_w2_bwd, gla_gate_b_bwd, gla_norm_g, gla_w_out, mix_ln_g, mix_ln_b, mlp_w1, mlp_w2,
           mlp_ln_g, mlp_ln_b):
    batch, seq_len, d_model = x.shape
    n = batch * seq_len
    x2 = x.reshape(n, d_model)

    inv_freq = ROPE_THETA ** (-jnp.arange(0, A_HEAD_DIM, 2, dtype=F32) / A_HEAD_DIM)
    ang_t = inv_freq[:, None] * positions.astype(F32).reshape(1, n)
    q_cols = A_HEADS * A_HEAD_DIM
    k_cols = A_KV_HEADS * A_HEAD_DIM

    for i in range(DEPTH):
        j = i // 2
        w1 = mlp_w1[i].astype(BF16)
        w2 = mlp_w2[i].astype(BF16)
        tail = (_row(mix_ln_g[i]), _row(mix_ln_b[i]), w1, w2, _row(mlp_ln_g[i]), _row(mlp_ln_b[i]))
        if i % 2 == 0:
            w_in = attn_w_in[j].astype(BF16)
            wqv_t = jnp.concatenate([w_in[:, :q_cols], w_in[:, q_cols + k_cols:]], axis=1).T
            qt, kd, vt = _attn_in_proj(x2, ang_t, wqv_t, w_in[:, q_cols:q_cols + k_cols])
            o = _window_attention(qt, kd, vt, attn_sink[j].astype(F32), batch, seq_len)
            x2 = _post_mixer(_attn_post_kernel, [o], [], x2, attn_w_out[j].astype(BF16), *tail)
        else:
            w_in = gla_w_in[j]
            lr_cols = 2 * B_GATE_RANK
            w_in = jnp.pad(w_in, ((0, 0), (0, LANES - lr_cols))).astype(BF16)
            v, r, *decayed, tot = _gla_in_proj(
                x2, w_in, _pad_gate_weight(gla_gate_w2_fwd[j], 0), _row(gla_gate_b_fwd[j]),
                _pad_gate_weight(gla_gate_w2_bwd[j], B_GATE_RANK), _row(gla_gate_b_bwd[j]))
            o_f, o_b = _gla_scan(v, decayed[:3], decayed[3:], tot, batch, seq_len)
            x2 = _post_mixer(_gla_post_kernel, [o_f, o_b, r], [_row(gla_norm_g[j])], x2,
                             gla_w_out[j].astype(BF16), *tail)
    return x2.reshape(batch, seq_len, d_model)
```

```python
import functools

import jax
import jax.numpy as jnp
from jax import lax
from jax.experimental import pallas as pl
from jax.experimental.pallas import tpu as pltpu

D_MODEL = 1024
DEPTH = 2

A_HEADS = 16
A_KV_HEADS = 4
A_HEAD_DIM = D_MODEL // A_HEADS
A_GROUP = A_HEADS // A_KV_HEADS
WINDOW = 128
ROPE_THETA = 10000.0

B_HEADS = 4
B_KEY_DIM = (D_MODEL // 2) // B_HEADS
B_VAL_DIM = D_MODEL // B_HEADS
B_GATE_RANK = 16
B_GATE_TAU = 16.0
B_CHUNK = 64
B_QK_COLS = B_HEADS * B_KEY_DIM
B_V_COLS = B_HEADS * B_VAL_DIM

D_FF = 4 * D_MODEL
DN_ALPHA = float((2 * DEPTH) ** 0.25)
LN_EPS = 1e-5
HEAD_NORM_EPS = 1e-6
LN_2 = 0.6931471805599453
LOG2_E = 1.4426950408889634

LANES = 128
BF16_ROWS = 16
V_ROWS = A_HEAD_DIM + BF16_ROWS
VMEM_LIMIT_BYTES = 56 * 1024 * 1024

TOKEN_TILE = 512
ATTN_Q_TILE = 1024
GLA_TILE = TOKEN_TILE
FF_CHUNK = 1024

BF16 = jnp.bfloat16
F32 = jnp.float32


def _const_spec(shape):
    zeros = (0,) * len(shape)
    return pl.BlockSpec(shape, lambda *_: zeros, pipeline_mode=pl.Buffered(1))


def _compiler_params(semantics, flags=None):
    return pltpu.CompilerParams(dimension_semantics=semantics, vmem_limit_bytes=VMEM_LIMIT_BYTES, flags=flags)


def _layer_norm(z, g, b):
    mu = jnp.mean(z, axis=-1, keepdims=True)
    zc = z - mu
    var = jnp.mean(zc * zc, axis=-1, keepdims=True)
    return zc * lax.rsqrt(var + LN_EPS) * g + b


def _lane_index(shape):
    return lax.broadcasted_iota(jnp.int32, shape, len(shape) - 1)


def _attn_in_kernel(x_ref, ang_ref, wqv_ref, wk_ref, qt_ref, k_ref, vt_ref):
    xb = x_ref[...].astype(BF16)
    half = A_HEAD_DIM // 2
    q_cols = A_HEADS * A_HEAD_DIM
    kv_cols = A_KV_HEADS * A_HEAD_DIM
    scale = A_HEAD_DIM ** -0.5 * LOG2_E
    block = 4 * A_HEAD_DIM

    def project_t(lo):
        return lax.dot_general(wqv_ref[lo:lo + block, :], xb, (((1,), (1,)), ((), ())),
                               preferred_element_type=F32)

    def rotate_q(hq, lo):
        for h in range(block // A_HEAD_DIM):
            r0 = h * A_HEAD_DIM
            t1 = hq[r0:r0 + half, :]
            t2 = hq[r0 + half:r0 + A_HEAD_DIM, :]
            qt_ref[lo + r0:lo + r0 + half, :] = ((t1 * cos_t - t2 * sin_t) * scale).astype(BF16)
            qt_ref[lo + r0 + half:lo + r0 + A_HEAD_DIM, :] = ((t2 * cos_t + t1 * sin_t) * scale).astype(BF16)

    def rotate_k(hk):
        cos = jnp.concatenate([cos_t] * (LANES // half), axis=0).T
        sin = jnp.concatenate([sin_t] * (LANES // half), axis=0).T
        lane = _lane_index(cos.shape)
        first_half = (lane % A_HEAD_DIM) < half
        sin_signed = jnp.where(first_half, -sin, sin)
        low_head = lane < A_HEAD_DIM
        for j in range(kv_cols // LANES):
            t = hk[:, j * LANES:(j + 1) * LANES]
            partner = jnp.where(first_half, pltpu.roll(t, LANES - half, 1), pltpu.roll(t, half, 1))
            t = t * cos + partner * sin_signed
            swapped = pltpu.roll(t, A_HEAD_DIM, 1)
            k_ref[:, (2 * j) * LANES:(2 * j + 1) * LANES] = jnp.where(low_head, t, swapped).astype(BF16)
            k_ref[:, (2 * j + 1) * LANES:(2 * j + 2) * LANES] = jnp.where(low_head, swapped, t).astype(BF16)

    hk = jnp.dot(xb, wk_ref[...], preferred_element_type=F32)
    cos_t = jnp.cos(ang_ref[...])
    sin_t = jnp.sin(ang_ref[...])
    pieces = {0: project_t(0)}
    rotate_k(hk)
    for p in range(1, q_cols // block):
        pieces[p] = project_t(p * block)
        rotate_q(pieces.pop(p - 1), (p - 1) * block)
    hv = project_t(q_cols)
    rotate_q(pieces.pop(q_cols // block - 1), q_cols - block)
    vt_ref[...] = hv.astype(BF16)


def _attn_in_proj(x2, ang_t, wqv_t, wk):
    n = x2.shape[0]
    tm = TOKEN_TILE
    q_cols = A_HEADS * A_HEAD_DIM
    kv_cols = A_KV_HEADS * A_HEAD_DIM
    k_dup = A_KV_HEADS * LANES
    return pl.pallas_call(
        _attn_in_kernel,
        grid=(n // tm,),
        in_specs=[
            pl.BlockSpec((tm, D_MODEL), lambda i: (i, 0)),
            pl.BlockSpec((ang_t.shape[0], tm), lambda i: (0, i)),
            _const_spec(wqv_t.shape),
            _const_spec(wk.shape),
        ],
        out_specs=[
            pl.BlockSpec((q_cols, tm), lambda i: (0, i)),
            pl.BlockSpec((tm, k_dup), lambda i: (i, 0)),
            pl.BlockSpec((kv_cols, tm), lambda i: (0, i)),
        ],
        out_shape=[
            jax.ShapeDtypeStruct((q_cols, n), BF16),
            jax.ShapeDtypeStruct((n, k_dup), BF16),
            jax.ShapeDtypeStruct((kv_cols, n), BF16),
        ],
        compiler_params=_compiler_params(("parallel",)),
        name="attn_in_proj",
    )(x2, ang_t, wqv_t, wk)


def _window_attn_kernel(sink_ref, qt_ref, kp_ref, km_ref, kn_ref, vp_ref, vm_ref, vn_ref, o_ref,
                        kbuf, vbuf):
    tq = ATTN_Q_TILE
    blk = WINDOW
    span = 3 * blk
    i = pl.program_id(1)
    last_tile = pl.num_programs(1) - 1
    kbuf[0:blk, :] = kp_ref[...]
    kbuf[blk:blk + tq, :] = km_ref[...]
    kbuf[blk + tq:, :] = kn_ref[...]
    for g in range(A_KV_HEADS):
        src = slice(g * A_HEAD_DIM, (g + 1) * A_HEAD_DIM)
        dst = slice(g * V_ROWS, g * V_ROWS + A_HEAD_DIM)
        vbuf[dst, 0:blk] = vp_ref[src, :]
        vbuf[dst, blk:blk + tq] = vm_ref[src, :]
        vbuf[dst, blk + tq:] = vn_ref[src, :]
        vbuf[g * V_ROWS + A_HEAD_DIM:(g + 1) * V_ROWS, :] = jnp.ones((BF16_ROWS, tq + 2 * blk), BF16)

    pair_shape = (blk, 2 * LANES)
    key = lax.broadcasted_iota(jnp.int32, pair_shape, 0)
    query = _lane_index(pair_shape) % blk
    neg_inf = jnp.float32(-jnp.inf)
    prev_bias = jnp.where(key >= query, 0.0, neg_inf)
    next_bias = jnp.where(key <= query, 0.0, neg_inf)
    first_prev_bias = jnp.where(i == 0, neg_inf, prev_bias)
    last_next_bias = jnp.where(i == last_tile, neg_inf, next_bias)
    low_rows = lax.broadcasted_iota(jnp.int32, (LANES, blk), 0) < A_HEAD_DIM
    zero = jnp.zeros((LANES, blk), BF16)

    n_blocks = tq // blk
    units = [(jj, g, pr) for jj in range(n_blocks) for g in range(A_KV_HEADS) for pr in range(A_GROUP // 2)]

    def pair_lanes(g, pr):
        return slice((g * 2 + pr) * LANES, (g * 2 + pr + 1) * LANES)

    def scores(jj, g, pr):
        kk = kbuf[jj * blk:jj * blk + span, g * LANES:(g + 1) * LANES]
        qt = qt_ref[pair_lanes(g, pr), jj * blk:(jj + 1) * blk]
        w = jnp.concatenate([jnp.where(low_rows, qt, zero), jnp.where(low_rows, zero, qt)], axis=1)
        return jnp.dot(kk, w, preferred_element_type=F32)

    def softmax(s, jj, g, pr):
        s_prev = s[0:blk] + (first_prev_bias if jj == 0 else prev_bias)
        s_mid = s[blk:2 * blk]
        s_next = s[2 * blk:] + (last_next_bias if jj == n_blocks - 1 else next_bias)
        head = g * A_GROUP + 2 * pr
        sink = jnp.concatenate([jnp.full((1, blk), sink_ref[head] * LOG2_E, F32),
                                jnp.full((1, blk), sink_ref[head + 1] * LOG2_E, F32)], axis=1)
        m = jnp.maximum(jnp.maximum(jnp.max(s_prev, axis=0, keepdims=True),
                                    jnp.max(s_mid, axis=0, keepdims=True)),
                        jnp.maximum(jnp.max(s_next, axis=0, keepdims=True), sink))
        p = jnp.concatenate([jnp.exp2(s_prev - m), jnp.exp2(s_mid - m), jnp.exp2(s_next - m)], axis=0)
        return p.astype(BF16), jnp.exp2(sink - m)

    def weighted_values(p, sink_term, jj, g, pr):
        vt = vbuf[g * V_ROWS:(g + 1) * V_ROWS, jj * blk:jj * blk + span]
        ot = jnp.dot(vt, p, preferred_element_type=F32)
        denom = ot[A_HEAD_DIM:A_HEAD_DIM + 1, :] + sink_term
        ot = ot[0:A_HEAD_DIM, :] * (1.0 / denom)
        both = jnp.concatenate([ot[:, 0:blk], ot[:, blk:]], axis=0)
        o_ref[jj * blk:(jj + 1) * blk, pair_lanes(g, pr)] = both.T.astype(BF16)

    s_stage = {}
    p_stage = {}
    for n in range(len(units) + 2):
        if n < len(units):
            s_stage[n] = scores(*units[n])
        if 0 <= n - 1 < len(units):
            p_stage[n - 1] = softmax(s_stage.pop(n - 1), *units[n - 1])
        if 0 <= n - 2 < len(units):
            weighted_values(*p_stage.pop(n - 2), *units[n - 2])


def _window_attention(qt, kd, vt, sink, batch, seq_len):
    q_cols, n = qt.shape
    v_cols = vt.shape[0]
    tq = ATTN_Q_TILE
    blk = WINDOW
    nt = seq_len // tq
    per_tile = tq // blk
    nblk = seq_len // blk
    k_dup = kd.shape[1]

    def main_idx(b, i):
        return b * nt + i

    def prev_idx(b, i):
        return b * nblk + jnp.maximum(i * per_tile - 1, 0)

    def next_idx(b, i):
        return b * nblk + jnp.minimum((i + 1) * per_tile, nblk - 1)

    k_specs = [pl.BlockSpec((blk, k_dup), lambda b, i: (prev_idx(b, i), 0)),
               pl.BlockSpec((tq, k_dup), lambda b, i: (main_idx(b, i), 0)),
               pl.BlockSpec((blk, k_dup), lambda b, i: (next_idx(b, i), 0))]
    v_specs = [pl.BlockSpec((v_cols, blk), lambda b, i: (0, prev_idx(b, i))),
               pl.BlockSpec((v_cols, tq), lambda b, i: (0, main_idx(b, i))),
               pl.BlockSpec((v_cols, blk), lambda b, i: (0, next_idx(b, i)))]
    return pl.pallas_call(
        _window_attn_kernel,
        grid=(batch, nt),
        in_specs=[pl.BlockSpec(memory_space=pltpu.SMEM),
                  pl.BlockSpec((q_cols, tq), lambda b, i: (0, main_idx(b, i)))] + k_specs + v_specs,
        out_specs=pl.BlockSpec((tq, q_cols), lambda b, i: (main_idx(b, i), 0)),
        out_shape=jax.ShapeDtypeStruct((n, q_cols), BF16),
        scratch_shapes=[pltpu.VMEM((tq + 2 * blk, k_dup), BF16),
                        pltpu.VMEM((A_KV_HEADS * V_ROWS, tq + 2 * blk), BF16)],
        compiler_params=_compiler_params(("parallel", "parallel")),
        name="window_attn",
    )(sink, qt, kd, kd, kd, vt, vt, vt)


def _residual_mlp_tail(mixer_out, x_ref, wo_ref, ln1g_ref, ln1b_ref, w1_ref, w2_ref, ln2g_ref, ln2b_ref,
                       out_ref):
    tm = x_ref.shape[0]
    halves = [slice(0, tm // 2), slice(tm // 2, tm)]
    n_ff = D_FF // FF_CHUNK

    def project(rows):
        return jnp.dot(mixer_out(rows), wo_ref[...], preferred_element_type=F32)

    def norm1(y, rows):
        x1 = _layer_norm(DN_ALPHA * x_ref[rows, :] + y, ln1g_ref[...], ln1b_ref[...])
        return x1, x1.astype(BF16)

    def mlp_chunk(x1b, acc, c):
        hc = jnp.dot(x1b, w1_ref[:, c * FF_CHUNK:(c + 1) * FF_CHUNK], preferred_element_type=F32)
        hc = jnp.square(jnp.maximum(hc, 0.0)).astype(BF16)
        part = jnp.dot(hc, w2_ref[c * FF_CHUNK:(c + 1) * FF_CHUNK, :], preferred_element_type=F32)
        return part if acc is None else acc + part

    def norm2(x1, acc, rows):
        out_ref[rows, :] = _layer_norm(DN_ALPHA * x1 + acc, ln2g_ref[...], ln2b_ref[...])

    a, b = halves
    x1_a, x1b_a = norm1(project(a), a)
    y_b = project(b)
    acc_a = mlp_chunk(x1b_a, None, 0)
    x1_b, x1b_b = norm1(y_b, b)
    for c in range(1, n_ff):
        acc_a = mlp_chunk(x1b_a, acc_a, c)
    acc_b = mlp_chunk(x1b_b, None, 0)
    norm2(x1_a, acc_a, a)
    for c in range(1, n_ff):
        acc_b = mlp_chunk(x1b_b, acc_b, c)
    norm2(x1_b, acc_b, b)


def _attn_post_kernel(o_ref, *tail_refs):
    _residual_mlp_tail(lambda rows: o_ref[rows, :], *tail_refs)


def _gla_post_kernel(of_ref, ob_ref, r_ref, ng_ref, *tail_refs):
    def mixer_out(rows):
        o = of_ref[rows, :].astype(F32) + ob_ref[rows, :].astype(F32)
        r = r_ref[rows, :]
        gate = r * (1.0 / (1.0 + jnp.exp(-r)))
        ng = ng_ref[...]
        heads = []
        for h in range(B_HEADS):
            oh = o[:, h * B_VAL_DIM:(h + 1) * B_VAL_DIM]
            oh = oh * lax.rsqrt(jnp.mean(oh * oh, axis=-1, keepdims=True) + HEAD_NORM_EPS)
            heads.append((oh * ng * gate[:, h * B_VAL_DIM:(h + 1) * B_VAL_DIM]).astype(BF16))
        return jnp.concatenate(heads, axis=-1)

    _residual_mlp_tail(mixer_out, *tail_refs)


def _post_mixer(kernel_fn, mixer_inputs, extra_consts, x2, wo, ln1g, ln1b, w1, w2, ln2g, ln2b):
    n = x2.shape[0]
    tm = TOKEN_TILE
    tile = lambda cols: pl.BlockSpec((tm, cols), lambda i: (i, 0))
    consts = list(extra_consts)
    return pl.pallas_call(
        kernel_fn,
        grid=(n // tm,),
        in_specs=[tile(a.shape[1]) for a in mixer_inputs]
        + [_const_spec(c.shape) for c in consts]
        + [tile(D_MODEL), _const_spec(wo.shape), _const_spec(ln1g.shape), _const_spec(ln1b.shape),
           _const_spec(w1.shape), _const_spec(w2.shape), _const_spec(ln2g.shape), _const_spec(ln2b.shape)],
        out_specs=tile(D_MODEL),
        out_shape=jax.ShapeDtypeStruct((n, D_MODEL), F32),
        compiler_params=_compiler_params(("parallel",)),
        name=kernel_fn.__name__.strip("_"),
    )(*mixer_inputs, *consts, x2, wo, ln1g, ln1b, w1, w2, ln2g, ln2b)


def _gla_in_kernel(x_ref, w_ref, w2f_ref, gbf_ref, w2b_ref, gbb_ref,
                   v_ref, r_ref, qef_ref, kef_ref, kdf_ref, qeb_ref, keb_ref, kdb_ref, tot_ref):
    xb = x_ref[...].astype(BF16)
    tm = x_ref.shape[0]
    c1 = B_QK_COLS
    c2 = 2 * B_QK_COLS
    c3 = c2 + B_V_COLS
    c4 = c3 + B_V_COLS
    n_chunks = tm // B_CHUNK

    def project(lo, hi):
        return jnp.dot(xb, w_ref[:, lo:hi], preferred_element_type=F32)

    def chunk_totals(b, reverse):
        edge = 0 if reverse else B_CHUNK - 1
        return [b[c * B_CHUNK + edge:c * B_CHUNK + edge + 1, :] for c in range(n_chunks)]

    def decayed(q, k, b, totals, qe_ref, ke_ref, kd_ref):
        decay_total = jnp.concatenate(
            [jnp.broadcast_to(jnp.exp(t), (B_CHUNK, t.shape[1])) for t in totals], axis=0)
        ke = k * jnp.exp(-b)
        qe_ref[...] = (q * jnp.exp(b)).astype(BF16)
        ke_ref[...] = ke.astype(BF16)
        kd_ref[...] = (ke * decay_total).astype(BF16)

    lr = project(c4, c4 + LANES).astype(BF16)
    q = project(0, c1) * (B_KEY_DIM ** -0.5)
    g_f = _log_gate(lr, w2f_ref[...], gbf_ref[...])
    k = project(c1, c2)
    g_b = _log_gate(lr, w2b_ref[...], gbb_ref[...])
    v_ref[...] = project(c2, c3).astype(BF16)
    b_f = _chunk_cumsum(g_f, False)
    totals_f = chunk_totals(b_f, False)
    decayed(q, k, b_f, totals_f, qef_ref, kef_ref, kdf_ref)
    b_b = _chunk_cumsum(g_b, True)
    r_ref[...] = project(c3, c4)
    totals_b = chunk_totals(b_b, True)
    decayed(q, k, b_b, totals_b, qeb_ref, keb_ref, kdb_ref)
    tot_ref[0] = jnp.concatenate(totals_f + totals_b, axis=0)


def _gla_in_proj(x2, w, w2f, gbf, w2b, gbb):
    n = x2.shape[0]
    tm = TOKEN_TILE
    n_tot = 2 * (tm // B_CHUNK)
    tile = lambda cols: pl.BlockSpec((tm, cols), lambda i: (i, 0))
    qk = jax.ShapeDtypeStruct((n, B_QK_COLS), BF16)
    return pl.pallas_call(
        _gla_in_kernel,
        grid=(n // tm,),
        in_specs=[tile(D_MODEL), _const_spec(w.shape), _const_spec(w2f.shape), _const_spec(gbf.shape),
                  _const_spec(w2b.shape), _const_spec(gbb.shape)],
        out_specs=[tile(B_V_COLS), tile(B_V_COLS)] + [tile(B_QK_COLS)] * 6
        + [pl.BlockSpec((1, n_tot, B_QK_COLS), lambda i: (i, 0, 0))],
        out_shape=[jax.ShapeDtypeStruct((n, B_V_COLS), BF16), jax.ShapeDtypeStruct((n, B_V_COLS), F32)]
        + [qk] * 6 + [jax.ShapeDtypeStruct((n // tm, n_tot, B_QK_COLS), F32)],
        compiler_params=_compiler_params(("parallel",)),
        name="gla_in_proj",
    )(x2, w, w2f, gbf, w2b, gbb)


def _log_gate(lr, w2, bias):
    z = jnp.dot(lr, w2, preferred_element_type=F32) + bias
    log_term = jnp.log2(1.0 + jnp.exp(-jnp.abs(z)))
    return jnp.minimum(z, 0.0) * (1.0 / B_GATE_TAU) - log_term * (LN_2 / B_GATE_TAU)


def _chunk_cumsum(g, reverse):
    hi = g.astype(BF16)
    rest = g - hi.astype(F32)
    mid = rest.astype(BF16)
    lo = (rest - mid.astype(F32)).astype(BF16)
    t_row = lax.broadcasted_iota(jnp.int32, (B_CHUNK, 3 * B_CHUNK), 0)
    t_col = lax.broadcasted_iota(jnp.int32, (B_CHUNK, 3 * B_CHUNK), 1) % B_CHUNK
    keep = (t_row <= t_col) if reverse else (t_row >= t_col)
    tri = jnp.where(keep, 1.0, 0.0).astype(BF16)
    out = []
    for c in range(g.shape[0] // B_CHUNK):
        rows = slice(c * B_CHUNK, (c + 1) * B_CHUNK)
        terms = jnp.concatenate([hi[rows], mid[rows], lo[rows]], axis=0)
        out.append(jnp.dot(tri, terms, preferred_element_type=F32))
    return jnp.concatenate(out, axis=0)


def _gla_scan_kernel(qef_ref, kef_ref, kdf_ref, vf_ref, totf_ref, qeb_ref, keb_ref, kdb_ref, vb_ref, totb_ref,
                     of_ref, ob_ref, d_ref, sf_ref, sb_ref):
    @pl.when(pl.program_id(1) == 0)
    def _():
        sf_ref[...] = jnp.zeros_like(sf_ref)
        sb_ref[...] = jnp.zeros_like(sb_ref)

    n_chunks = GLA_TILE // B_CHUNK
    pair_k = 2 * B_KEY_DIM
    totals = jnp.concatenate([totf_ref[0, 0:n_chunks, :], totb_ref[0, n_chunks:, :]], axis=0)
    d_ref[...] = jnp.exp(totals).T

    directions = ((False, qef_ref, kef_ref, kdf_ref, vf_ref, 0, of_ref, sf_ref),
                  (True, qeb_ref, keb_ref, kdb_ref, vb_ref, n_chunks, ob_ref, sb_ref))
    t_row = lax.broadcasted_iota(jnp.int32, (B_CHUNK, 2 * B_CHUNK), 0)
    t_col = _lane_index((B_CHUNK, 2 * B_CHUNK)) % B_CHUNK
    first_head_lanes = _lane_index((B_CHUNK, pair_k)) < B_KEY_DIM
    zero_k = jnp.zeros((B_CHUNK, pair_k), BF16)
    zero_v = jnp.zeros((B_CHUNK, B_VAL_DIM), BF16)
    units = [(c, d, hp) for c in range(n_chunks) for d in range(2) for hp in range(B_HEADS // 2)]

    def chunk_rows(c, reverse):
        cc = n_chunks - 1 - c if reverse else c
        return cc, slice(cc * B_CHUNK, (cc + 1) * B_CHUNK)

    def prepare(c, d, hp):
        reverse, qe_ref, ke_ref, kd_ref, v_ref = directions[d][:5]
        _, rows = chunk_rows(c, reverse)
        cols = slice(hp * pair_k, (hp + 1) * pair_k)
        qe = qe_ref[rows, cols]
        ke = ke_ref[rows, cols]
        kd = kd_ref[rows, cols]
        ke_bd = jnp.concatenate([jnp.where(first_head_lanes, ke, zero_k),
                                 jnp.where(first_head_lanes, zero_k, ke)], axis=0)
        att = lax.dot_general(qe, ke_bd, (((1,), (1,)), ((), ())), preferred_element_type=F32)
        causal = (t_row <= t_col) if reverse else (t_row >= t_col)
        att = jnp.where(causal, att, 0.0).astype(BF16)
        updates = []
        for hh in range(2):
            h = 2 * hp + hh
            vh = v_ref[rows, h * B_VAL_DIM:(h + 1) * B_VAL_DIM]
            updates.append(lax.dot_general(kd[:, hh * B_KEY_DIM:(hh + 1) * B_KEY_DIM], vh,
                                           (((0,), (0,)), ((), ())), preferred_element_type=F32))
        return qe, att, updates

    def apply(prepared, c, d, hp):
        reverse, _, _, _, v_ref, d_col0, o_ref, s_ref = directions[d]
        qe, att, updates = prepared
        cc, rows = chunk_rows(c, reverse)
        for hh in range(2):
            h = 2 * hp + hh
            vs = slice(h * B_VAL_DIM, (h + 1) * B_VAL_DIM)
            vh = v_ref[rows, vs]
            state = s_ref[h]
            lhs = jnp.concatenate([qe[:, hh * B_KEY_DIM:(hh + 1) * B_KEY_DIM], att], axis=1)
            rhs = jnp.concatenate([state.astype(BF16)] + ([vh, zero_v] if hh == 0 else [zero_v, vh]), axis=0)
            o_ref[rows, vs] = jnp.dot(lhs, rhs, preferred_element_type=F32).astype(o_ref.dtype)
            decay = d_ref[h * B_KEY_DIM:(h + 1) * B_KEY_DIM, d_col0 + cc:d_col0 + cc + 1]
            s_ref[h] = state * decay + updates[hh]

    prepared = {}
    for n in range(len(units) + 1):
        if n < len(units):
            prepared[n] = prepare(*units[n])
        if n >= 1:
            apply(prepared.pop(n - 1), *units[n - 1])


def _gla_scan(v, decayed_f, decayed_b, tot, batch, seq_len):
    n = v.shape[0]
    ts = GLA_TILE
    nt = seq_len // ts
    n_tot = tot.shape[1]

    def fwd_tile(b, i):
        return b * nt + i

    def bwd_tile(b, i):
        return b * nt + nt - 1 - i

    def token_specs(tile_of):
        rows = lambda b, i: (tile_of(b, i), 0)
        return [pl.BlockSpec((ts, B_QK_COLS), rows)] * 3 + [
            pl.BlockSpec((ts, B_V_COLS), rows),
            pl.BlockSpec((1, n_tot, B_QK_COLS), lambda b, i: (tile_of(b, i), 0, 0))]

    return pl.pallas_call(
        _gla_scan_kernel,
        grid=(batch, nt),
        in_specs=token_specs(fwd_tile) + token_specs(bwd_tile),
        out_specs=[pl.BlockSpec((ts, B_V_COLS), lambda b, i: (fwd_tile(b, i), 0)),
                   pl.BlockSpec((ts, B_V_COLS), lambda b, i: (bwd_tile(b, i), 0))],
        out_shape=[jax.ShapeDtypeStruct((n, B_V_COLS), BF16), jax.ShapeDtypeStruct((n, B_V_COLS), BF16)],
        scratch_shapes=[
            pltpu.VMEM((B_QK_COLS, n_tot), F32),
            pltpu.VMEM((B_HEADS, B_KEY_DIM, B_VAL_DIM), F32),
            pltpu.VMEM((B_HEADS, B_KEY_DIM, B_VAL_DIM), F32),
        ],
        compiler_params=_compiler_params(("parallel", "arbitrary")),
        name="gla_scan",
    )(*decayed_f, v, tot, *decayed_b, v, tot)


def _row(v):
    return v.reshape(1, -1).astype(F32)


def _pad_gate_weight(w2, first_row):
    out = jnp.zeros((LANES, w2.shape[1]), BF16)
    return lax.dynamic_update_slice(out, w2.astype(BF16), (first_row, 0))


def kernel(x, positions, attn_w_in, attn_sink, attn_w_out, gla_w_in, gla_gate_w2_fwd, gla_gate_b_fwd,
           gla_gate_w2_bwd, gla_gate_b_bwd, gla_norm_g, gla_w_out, mix_ln_g, mix_ln_b, mlp_w1, mlp_w2,
           mlp_ln_g, mlp_ln_b):
    batch, seq_len, d_model = x.shape
    n = batch * seq_len
    x2 = x.reshape(n, d_model)

    inv_freq = ROPE_THETA ** (-jnp.arange(0, A_HEAD_DIM, 2, dtype=F32) / A_HEAD_DIM)
    ang_t = inv_freq[:, None] * positions.astype(F32).reshape(1, n)
    q_cols = A_HEADS * A_HEAD_DIM
    k_cols = A_KV_HEADS * A_HEAD_DIM

    for i in range(DEPTH):
        j = i // 2
        w1 = mlp_w1[i].astype(BF16)
        w2 = mlp_w2[i].astype(BF16)
        tail = (_row(mix_ln_g[i]), _row(mix_ln_b[i]), w1, w2, _row(mlp_ln_g[i]), _row(mlp_ln_b[i]))
        if i % 2 == 0:
            w_in = attn_w_in[j].astype(BF16)
            wqv_t = jnp.concatenate([w_in[:, :q_cols], w_in[:, q_cols + k_cols:]], axis=1).T
            qt, kd, vt = _attn_in_proj(x2, ang_t, wqv_t, w_in[:, q_cols:q_cols + k_cols])
            o = _window_attention(qt, kd, vt, attn_sink[j].astype(F32), batch, seq_len)
            x2 = _post_mixer(_attn_post_kernel, [o], [], x2, attn_w_out[j].astype(BF16), *tail)
        else:
            w_in = gla_w_in[j]
            lr_cols = 2 * B_GATE_RANK
            w_in = jnp.pad(w_in, ((0, 0), (0, LANES - lr_cols))).astype(BF16)
            v, r, *decayed, tot = _gla_in_proj(
                x2, w_in, _pad_gate_weight(gla_gate_w2_fwd[j], 0), _row(gla_gate_b_fwd[j]),
                _pad_gate_weight(gla_gate_w2_bwd[j], B_GATE_RANK), _row(gla_gate_b_bwd[j]))
            o_f, o_b = _gla_scan(v, decayed[:3], decayed[3:], tot, batch, seq_len)
            x2 = _post_mixer(_gla_post_kernel, [o_f, o_b, r], [_row(gla_norm_g[j])], x2,
                             gla_w_out[j].astype(BF16), *tail)
    return x2.reshape(batch, seq_len, d_model)
```

```python
import functools

import jax
import jax.numpy as jnp
from jax import lax
from jax.experimental import pallas as pl
from jax.experimental.pallas import tpu as pltpu

D_MODEL = 1024
DEPTH = 2

A_HEADS = 16
A_KV_HEADS = 4
A_HEAD_DIM = D_MODEL // A_HEADS
A_GROUP = A_HEADS // A_KV_HEADS
WINDOW = 128
ROPE_THETA = 10000.0

B_HEADS = 4
B_KEY_DIM = (D_MODEL // 2) // B_HEADS
B_VAL_DIM = D_MODEL // B_HEADS
B_GATE_RANK = 16
B_GATE_TAU = 16.0
B_CHUNK = 64
B_QK_COLS = B_HEADS * B_KEY_DIM
B_V_COLS = B_HEADS * B_VAL_DIM

D_FF = 4 * D_MODEL
DN_ALPHA = float((2 * DEPTH) ** 0.25)
LN_EPS = 1e-5
HEAD_NORM_EPS = 1e-6
LN_2 = 0.6931471805599453
LOG2_E = 1.4426950408889634

LANES = 128
BF16_ROWS = 16
V_ROWS = A_HEAD_DIM + BF16_ROWS
VMEM_LIMIT_BYTES = 56 * 1024 * 1024

TOKEN_TILE = 512
ATTN_Q_TILE = 1024
GLA_TILE = TOKEN_TILE
FF_CHUNK = 1024

BF16 = jnp.bfloat16
F32 = jnp.float32


def _const_spec(shape):
    zeros = (0,) * len(shape)
    return pl.BlockSpec(shape, lambda *_: zeros, pipeline_mode=pl.Buffered(1))


def _compiler_params(semantics, flags=None):
    return pltpu.CompilerParams(dimension_semantics=semantics, vmem_limit_bytes=VMEM_LIMIT_BYTES, flags=flags)


def _layer_norm(z, g, b):
    mu = jnp.mean(z, axis=-1, keepdims=True)
    zc = z - mu
    var = jnp.mean(zc * zc, axis=-1, keepdims=True)
    return zc * lax.rsqrt(var + LN_EPS) * g + b


def _lane_index(shape):
    return lax.broadcasted_iota(jnp.int32, shape, len(shape) - 1)


def _attn_in_kernel(x_ref, ang_ref, wqv_ref, wk_ref, qt_ref, k_ref, vt_ref):
    xb = x_ref[...].astype(BF16)
    half = A_HEAD_DIM // 2
    q_cols = A_HEADS * A_HEAD_DIM
    kv_cols = A_KV_HEADS * A_HEAD_DIM
    scale = A_HEAD_DIM ** -0.5 * LOG2_E
    block = 4 * A_HEAD_DIM

    def project_t(lo):
        return lax.dot_general(wqv_ref[lo:lo + block, :], xb, (((1,), (1,)), ((), ())),
                               preferred_element_type=F32)

    def rotate_q(hq, lo):
        for h in range(block // A_HEAD_DIM):
            r0 = h * A_HEAD_DIM
            t1 = hq[r0:r0 + half, :]
            t2 = hq[r0 + half:r0 + A_HEAD_DIM, :]
            qt_ref[lo + r0:lo + r0 + half, :] = ((t1 * cos_t - t2 * sin_t) * scale).astype(BF16)
            qt_ref[lo + r0 + half:lo + r0 + A_HEAD_DIM, :] = ((t2 * cos_t + t1 * sin_t) * scale).astype(BF16)

    def rotate_k(hk):
        cos = jnp.concatenate([cos_t] * (LANES // half), axis=0).T
        sin = jnp.concatenate([sin_t] * (LANES // half), axis=0).T
        lane = _lane_index(cos.shape)
        first_half = (lane % A_HEAD_DIM) < half
        sin_signed = jnp.where(first_half, -sin, sin)
        low_head = lane < A_HEAD_DIM
        for j in range(kv_cols // LANES):
            t = hk[:, j * LANES:(j + 1) * LANES]
            partner = jnp.where(first_half, pltpu.roll(t, LANES - half, 1), pltpu.roll(t, half, 1))
            t = t * cos + partner * sin_signed
            swapped = pltpu.roll(t, A_HEAD_DIM, 1)
            k_ref[:, (2 * j) * LANES:(2 * j + 1) * LANES] = jnp.where(low_head, t, swapped).astype(BF16)
            k_ref[:, (2 * j + 1) * LANES:(2 * j + 2) * LANES] = jnp.where(low_head, swapped, t).astype(BF16)

    hk = jnp.dot(xb, wk_ref[...], preferred_element_type=F32)
    cos_t = jnp.cos(ang_ref[...])
    sin_t = jnp.sin(ang_ref[...])
    pieces = {0: project_t(0)}
    rotate_k(hk)
    for p in range(1, q_cols // block):
        pieces[p] = project_t(p * block)
        rotate_q(pieces.pop(p - 1), (p - 1) * block)
    hv = project_t(q_cols)
    rotate_q(pieces.pop(q_cols // block - 1), q_cols - block)
    vt_ref[...] = hv.astype(BF16)


def _attn_in_proj(x2, ang_t, wqv_t, wk):
    n = x2.shape[0]
    tm = TOKEN_TILE
    q_cols = A_HEADS * A_HEAD_DIM
    kv_cols = A_KV_HEADS * A_HEAD_DIM
    k_dup = A_KV_HEADS * LANES
    return pl.pallas_call(
        _attn_in_kernel,
        grid=(n // tm,),
        in_specs=[
            pl.BlockSpec((tm, D_MODEL), lambda i: (i, 0)),
            pl.BlockSpec((ang_t.shape[0], tm), lambda i: (0, i)),
            _const_spec(wqv_t.shape),
            _const_spec(wk.shape),
        ],
        out_specs=[
            pl.BlockSpec((q_cols, tm), lambda i: (0, i)),
            pl.BlockSpec((tm, k_dup), lambda i: (i, 0)),
            pl.BlockSpec((kv_cols, tm), lambda i: (0, i)),
        ],
        out_shape=[
            jax.ShapeDtypeStruct((q_cols, n), BF16),
            jax.ShapeDtypeStruct((n, k_dup), BF16),
            jax.ShapeDtypeStruct((kv_cols, n), BF16),
        ],
        compiler_params=_compiler_params(("parallel",)),
        name="attn_in_proj",
    )(x2, ang_t, wqv_t, wk)


def _window_attn_kernel(sink_ref, qt_ref, kp_ref, km_ref, kn_ref, vp_ref, vm_ref, vn_ref, o_ref,
                        kbuf, vbuf):
    tq = ATTN_Q_TILE
    blk = WINDOW
    span = 3 * blk
    i = pl.program_id(1)
    last_tile = pl.num_programs(1) - 1
    kbuf[0:blk, :] = kp_ref[...]
    kbuf[blk:blk + tq, :] = km_ref[...]
    kbuf[blk + tq:, :] = kn_ref[...]
    for g in range(A_KV_HEADS):
        src = slice(g * A_HEAD_DIM, (g + 1) * A_HEAD_DIM)
        dst = slice(g * V_ROWS, g * V_ROWS + A_HEAD_DIM)
        vbuf[dst, 0:blk] = vp_ref[src, :]
        vbuf[dst, blk:blk + tq] = vm_ref[src, :]
        vbuf[dst, blk + tq:] = vn_ref[src, :]
        vbuf[g * V_ROWS + A_HEAD_DIM:(g + 1) * V_ROWS, :] = jnp.ones((BF16_ROWS, tq + 2 * blk), BF16)

    pair_shape = (blk, 2 * LANES)
    key = lax.broadcasted_iota(jnp.int32, pair_shape, 0)
    query = _lane_index(pair_shape) % blk
    neg_inf = jnp.float32(-jnp.inf)
    prev_bias = jnp.where(key >= query, 0.0, neg_inf)
    next_bias = jnp.where(key <= query, 0.0, neg_inf)
    first_prev_bias = jnp.where(i == 0, neg_inf, prev_bias)
    last_next_bias = jnp.where(i == last_tile, neg_inf, next_bias)
    low_rows = lax.broadcasted_iota(jnp.int32, (LANES, blk), 0) < A_HEAD_DIM
    zero = jnp.zeros((LANES, blk), BF16)

    n_blocks = tq // blk
    units = [(jj, g, pr) for jj in range(n_blocks) for g in range(A_KV_HEADS) for pr in range(A_GROUP // 2)]

    def pair_lanes(g, pr):
        return slice((g * 2 + pr) * LANES, (g * 2 + pr + 1) * LANES)

    def scores(jj, g, pr):
        kk = kbuf[jj * blk:jj * blk + span, g * LANES:(g + 1) * LANES]
        qt = qt_ref[pair_lanes(g, pr), jj * blk:(jj + 1) * blk]
        w = jnp.concatenate([jnp.where(low_rows, qt, zero), jnp.where(low_rows, zero, qt)], axis=1)
        return jnp.dot(kk, w, preferred_element_type=F32)

    def softmax(s, jj, g, pr):
        s_prev = s[0:blk] + (first_prev_bias if jj == 0 else prev_bias)
        s_mid = s[blk:2 * blk]
        s_next = s[2 * blk:] + (last_next_bias if jj == n_blocks - 1 else next_bias)
        head = g * A_GROUP + 2 * pr
        sink = jnp.concatenate([jnp.full((1, blk), sink_ref[head] * LOG2_E, F32),
                                jnp.full((1, blk), sink_ref[head + 1] * LOG2_E, F32)], axis=1)
        m = jnp.maximum(jnp.maximum(jnp.max(s_prev, axis=0, keepdims=True),
                                    jnp.max(s_mid, axis=0, keepdims=True)),
                        jnp.maximum(jnp.max(s_next, axis=0, keepdims=True), sink))
        p = jnp.concatenate([jnp.exp2(s_prev - m), jnp.exp2(s_mid - m), jnp.exp2(s_next - m)], axis=0)
        return p.astype(BF16), jnp.exp2(sink - m)

    def weighted_values(p, sink_term, jj, g, pr):
        vt = vbuf[g * V_ROWS:(g + 1) * V_ROWS, jj * blk:jj * blk + span]
        ot = jnp.dot(vt, p, preferred_element_type=F32)
        denom = ot[A_HEAD_DIM:A_HEAD_DIM + 1, :] + sink_term
        ot = ot[0:A_HEAD_DIM, :] * (1.0 / denom)
        both = jnp.concatenate([ot[:, 0:blk], ot[:, blk:]], axis=0)
        o_ref[jj * blk:(jj + 1) * blk, pair_lanes(g, pr)] = both.T.astype(BF16)

    s_stage = {}
    p_stage = {}
    width = 4
    n_steps = len(units) // width
    for n in range(n_steps + 2):
        for u in range(width):
            if n < n_steps:
                s_stage[n * width + u] = scores(*units[n * width + u])
        for u in range(width):
            if 0 <= n - 1 < n_steps:
                i1 = (n - 1) * width + u
                p_stage[i1] = softmax(s_stage.pop(i1), *units[i1])
        for u in range(width):
            if 0 <= n - 2 < n_steps:
                i2 = (n - 2) * width + u
                weighted_values(*p_stage.pop(i2), *units[i2])


def _window_attention(qt, kd, vt, sink, batch, seq_len):
    q_cols, n = qt.shape
    v_cols = vt.shape[0]
    tq = ATTN_Q_TILE
    blk = WINDOW
    nt = seq_len // tq
    per_tile = tq // blk
    nblk = seq_len // blk
    k_dup = kd.shape[1]

    def main_idx(b, i):
        return b * nt + i

    def prev_idx(b, i):
        return b * nblk + jnp.maximum(i * per_tile - 1, 0)

    def next_idx(b, i):
        return b * nblk + jnp.minimum((i + 1) * per_tile, nblk - 1)

    k_specs = [pl.BlockSpec((blk, k_dup), lambda b, i: (prev_idx(b, i), 0)),
               pl.BlockSpec((tq, k_dup), lambda b, i: (main_idx(b, i), 0)),
               pl.BlockSpec((blk, k_dup), lambda b, i: (next_idx(b, i), 0))]
    v_specs = [pl.BlockSpec((v_cols, blk), lambda b, i: (0, prev_idx(b, i))),
               pl.BlockSpec((v_cols, tq), lambda b, i: (0, main_idx(b, i))),
               pl.BlockSpec((v_cols, blk), lambda b, i: (0, next_idx(b, i)))]
    return pl.pallas_call(
        _window_attn_kernel,
        grid=(batch, nt),
        in_specs=[pl.BlockSpec(memory_space=pltpu.SMEM),
                  pl.BlockSpec((q_cols, tq), lambda b, i: (0, main_idx(b, i)))] + k_specs + v_specs,
        out_specs=pl.BlockSpec((tq, q_cols), lambda b, i: (main_idx(b, i), 0)),
        out_shape=jax.ShapeDtypeStruct((n, q_cols), BF16),
        scratch_shapes=[pltpu.VMEM((tq + 2 * blk, k_dup), BF16),
                        pltpu.VMEM((A_KV_HEADS * V_ROWS, tq + 2 * blk), BF16)],
        compiler_params=_compiler_params(("parallel", "parallel")),
        name="window_attn",
    )(sink, qt, kd, kd, kd, vt, vt, vt)


def _residual_mlp_tail(mixer_out, x_ref, wo_ref, ln1g_ref, ln1b_ref, w1_ref, w2_ref, ln2g_ref, ln2b_ref,
                       out_ref):
    tm = x_ref.shape[0]
    halves = [slice(0, tm // 2), slice(tm // 2, tm)]
    n_ff = D_FF // FF_CHUNK

    def project(rows):
        return jnp.dot(mixer_out(rows), wo_ref[...], preferred_element_type=F32)

    def norm1(y, rows):
        x1 = _layer_norm(DN_ALPHA * x_ref[rows, :] + y, ln1g_ref[...], ln1b_ref[...])
        return x1, x1.astype(BF16)

    def mlp_chunk(x1b, acc, c):
        hc = jnp.dot(x1b, w1_ref[:, c * FF_CHUNK:(c + 1) * FF_CHUNK], preferred_element_type=F32)
        hc = jnp.square(jnp.maximum(hc, 0.0)).astype(BF16)
        part = jnp.dot(hc, w2_ref[c * FF_CHUNK:(c + 1) * FF_CHUNK, :], preferred_element_type=F32)
        return part if acc is None else acc + part

    def norm2(x1, acc, rows):
        out_ref[rows, :] = _layer_norm(DN_ALPHA * x1 + acc, ln2g_ref[...], ln2b_ref[...])

    a, b = halves
    x1_a, x1b_a = norm1(project(a), a)
    y_b = project(b)
    acc_a = mlp_chunk(x1b_a, None, 0)
    x1_b, x1b_b = norm1(y_b, b)
    for c in range(1, n_ff):
        acc_a = mlp_chunk(x1b_a, acc_a, c)
    acc_b = mlp_chunk(x1b_b, None, 0)
    norm2(x1_a, acc_a, a)
    for c in range(1, n_ff):
        acc_b = mlp_chunk(x1b_b, acc_b, c)
    norm2(x1_b, acc_b, b)


def _attn_post_kernel(o_ref, *tail_refs):
    _residual_mlp_tail(lambda rows: o_ref[rows, :], *tail_refs)


def _gla_post_kernel(of_ref, ob_ref, r_ref, ng_ref, *tail_refs):
    def mixer_out(rows):
        o = of_ref[rows, :].astype(F32) + ob_ref[rows, :].astype(F32)
        r = r_ref[rows, :]
        gate = r * (1.0 / (1.0 + jnp.exp(-r)))
        ng = ng_ref[...]
        heads = []
        for h in range(B_HEADS):
            oh = o[:, h * B_VAL_DIM:(h + 1) * B_VAL_DIM]
            oh = oh * lax.rsqrt(jnp.mean(oh * oh, axis=-1, keepdims=True) + HEAD_NORM_EPS)
            heads.append((oh * ng * gate[:, h * B_VAL_DIM:(h + 1) * B_VAL_DIM]).astype(BF16))
        return jnp.concatenate(heads, axis=-1)

    _residual_mlp_tail(mixer_out, *tail_refs)


def _post_mixer(kernel_fn, mixer_inputs, extra_consts, x2, wo, ln1g, ln1b, w1, w2, ln2g, ln2b):
    n = x2.shape[0]
    tm = TOKEN_TILE
    tile = lambda cols: pl.BlockSpec((tm, cols), lambda i: (i, 0))
    consts = list(extra_consts)
    return pl.pallas_call(
        kernel_fn,
        grid=(n // tm,),
        in_specs=[tile(a.shape[1]) for a in mixer_inputs]
        + [_const_spec(c.shape) for c in consts]
        + [tile(D_MODEL), _const_spec(wo.shape), _const_spec(ln1g.shape), _const_spec(ln1b.shape),
           _const_spec(w1.shape), _const_spec(w2.shape), _const_spec(ln2g.shape), _const_spec(ln2b.shape)],
        out_specs=tile(D_MODEL),
        out_shape=jax.ShapeDtypeStruct((n, D_MODEL), F32),
        compiler_params=_compiler_params(("parallel",)),
        name=kernel_fn.__name__.strip("_"),
    )(*mixer_inputs, *consts, x2, wo, ln1g, ln1b, w1, w2, ln2g, ln2b)


def _gla_in_kernel(x_ref, w_ref, w2f_ref, gbf_ref, w2b_ref, gbb_ref,
                   v_ref, r_ref, qef_ref, kef_ref, kdf_ref, qeb_ref, keb_ref, kdb_ref, tot_ref):
    xb = x_ref[...].astype(BF16)
    tm = x_ref.shape[0]
    c1 = B_QK_COLS
    c2 = 2 * B_QK_COLS
    c3 = c2 + B_V_COLS
    c4 = c3 + B_V_COLS
    n_chunks = tm // B_CHUNK

    def project(lo, hi):
        return jnp.dot(xb, w_ref[:, lo:hi], preferred_element_type=F32)

    def chunk_totals(b, reverse):
        edge = 0 if reverse else B_CHUNK - 1
        return [b[c * B_CHUNK + edge:c * B_CHUNK + edge + 1, :] for c in range(n_chunks)]

    def decayed(q, k, b, totals, qe_ref, ke_ref, kd_ref):
        decay_total = jnp.concatenate(
            [jnp.broadcast_to(jnp.exp(t), (B_CHUNK, t.shape[1])) for t in totals], axis=0)
        ke = k * jnp.exp(-b)
        qe_ref[...] = (q * jnp.exp(b)).astype(BF16)
        ke_ref[...] = ke.astype(BF16)
        kd_ref[...] = (ke * decay_total).astype(BF16)

    lr = project(c4, c4 + LANES).astype(BF16)
    q = project(0, c1) * (B_KEY_DIM ** -0.5)
    g_f = _log_gate(lr, w2f_ref[...], gbf_ref[...])
    k = project(c1, c2)
    g_b = _log_gate(lr, w2b_ref[...], gbb_ref[...])
    v_ref[...] = project(c2, c3).astype(BF16)
    b_f = _chunk_cumsum(g_f, False)
    totals_f = chunk_totals(b_f, False)
    decayed(q, k, b_f, totals_f, qef_ref, kef_ref, kdf_ref)
    b_b = _chunk_cumsum(g_b, True)
    r_ref[...] = project(c3, c4)
    totals_b = chunk_totals(b_b, True)
    decayed(q, k, b_b, totals_b, qeb_ref, keb_ref, kdb_ref)
    tot_ref[0] = jnp.concatenate(totals_f + totals_b, axis=0)


def _gla_in_proj(x2, w, w2f, gbf, w2b, gbb):
    n = x2.shape[0]
    tm = TOKEN_TILE
    n_tot = 2 * (tm // B_CHUNK)
    tile = lambda cols: pl.BlockSpec((tm, cols), lambda i: (i, 0))
    qk = jax.ShapeDtypeStruct((n, B_QK_COLS), BF16)
    return pl.pallas_call(
        _gla_in_kernel,
        grid=(n // tm,),
        in_specs=[tile(D_MODEL), _const_spec(w.shape), _const_spec(w2f.shape), _const_spec(gbf.shape),
                  _const_spec(w2b.shape), _const_spec(gbb.shape)],
        out_specs=[tile(B_V_COLS), tile(B_V_COLS)] + [tile(B_QK_COLS)] * 6
        + [pl.BlockSpec((1, n_tot, B_QK_COLS), lambda i: (i, 0, 0))],
        out_shape=[jax.ShapeDtypeStruct((n, B_V_COLS), BF16), jax.ShapeDtypeStruct((n, B_V_COLS), F32)]
        + [qk] * 6 + [jax.ShapeDtypeStruct((n // tm, n_tot, B_QK_COLS), F32)],
        compiler_params=_compiler_params(("parallel",)),
        name="gla_in_proj",
    )(x2, w, w2f, gbf, w2b, gbb)


def _log_gate(lr, w2, bias):
    z = jnp.dot(lr, w2, preferred_element_type=F32) + bias
    log_term = jnp.log2(1.0 + jnp.exp(-jnp.abs(z)))
    return jnp.minimum(z, 0.0) * (1.0 / B_GATE_TAU) - log_term * (LN_2 / B_GATE_TAU)


def _chunk_cumsum(g, reverse):
    hi = g.astype(BF16)
    rest = g - hi.astype(F32)
    mid = rest.astype(BF16)
    lo = (rest - mid.astype(F32)).astype(BF16)
    t_row = lax.broadcasted_iota(jnp.int32, (B_CHUNK, 3 * B_CHUNK), 0)
    t_col = lax.broadcasted_iota(jnp.int32, (B_CHUNK, 3 * B_CHUNK), 1) % B_CHUNK
    keep = (t_row <= t_col) if reverse else (t_row >= t_col)
    tri = jnp.where(keep, 1.0, 0.0).astype(BF16)
    out = []
    for c in range(g.shape[0] // B_CHUNK):
        rows = slice(c * B_CHUNK, (c + 1) * B_CHUNK)
        terms = jnp.concatenate([hi[rows], mid[rows], lo[rows]], axis=0)
        out.append(jnp.dot(tri, terms, preferred_element_type=F32))
    return jnp.concatenate(out, axis=0)


def _gla_scan_kernel(qef_ref, kef_ref, kdf_ref, vf_ref, totf_ref, qeb_ref, keb_ref, kdb_ref, vb_ref, totb_ref,
                     of_ref, ob_ref, d_ref, sf_ref, sb_ref):
    @pl.when(pl.program_id(1) == 0)
    def _():
        sf_ref[...] = jnp.zeros_like(sf_ref)
        sb_ref[...] = jnp.zeros_like(sb_ref)

    n_chunks = GLA_TILE // B_CHUNK
    pair_k = 2 * B_KEY_DIM
    totals = jnp.concatenate([totf_ref[0, 0:n_chunks, :], totb_ref[0, n_chunks:, :]], axis=0)
    d_ref[...] = jnp.exp(totals).T

    directions = ((False, qef_ref, kef_ref, kdf_ref, vf_ref, 0, of_ref, sf_ref),
                  (True, qeb_ref, keb_ref, kdb_ref, vb_ref, n_chunks, ob_ref, sb_ref))
    t_row = lax.broadcasted_iota(jnp.int32, (B_CHUNK, 2 * B_CHUNK), 0)
    t_col = _lane_index((B_CHUNK, 2 * B_CHUNK)) % B_CHUNK
    first_head_lanes = _lane_index((B_CHUNK, pair_k)) < B_KEY_DIM
    zero_k = jnp.zeros((B_CHUNK, pair_k), BF16)
    zero_v = jnp.zeros((B_CHUNK, B_VAL_DIM), BF16)
    units = [(c, d, hp) for c in range(n_chunks) for d in range(2) for hp in range(B_HEADS // 2)]

    def chunk_rows(c, reverse):
        cc = n_chunks - 1 - c if reverse else c
        return cc, slice(cc * B_CHUNK, (cc + 1) * B_CHUNK)

    def prepare(c, d, hp):
        reverse, qe_ref, ke_ref, kd_ref, v_ref = directions[d][:5]
        _, rows = chunk_rows(c, reverse)
        cols = slice(hp * pair_k, (hp + 1) * pair_k)
        qe = qe_ref[rows, cols]
        ke = ke_ref[rows, cols]
        kd = kd_ref[rows, cols]
        ke_bd = jnp.concatenate([jnp.where(first_head_lanes, ke, zero_k),
                                 jnp.where(first_head_lanes, zero_k, ke)], axis=0)
        att = lax.dot_general(qe, ke_bd, (((1,), (1,)), ((), ())), preferred_element_type=F32)
        causal = (t_row <= t_col) if reverse else (t_row >= t_col)
        att = jnp.where(causal, att, 0.0).astype(BF16)
        updates = []
        for hh in range(2):
            h = 2 * hp + hh
            vh = v_ref[rows, h * B_VAL_DIM:(h + 1) * B_VAL_DIM]
            updates.append(lax.dot_general(kd[:, hh * B_KEY_DIM:(hh + 1) * B_KEY_DIM], vh,
                                           (((0,), (0,)), ((), ())), preferred_element_type=F32))
        return qe, att, updates

    def apply(prepared, c, d, hp):
        reverse, _, _, _, v_ref, d_col0, o_ref, s_ref = directions[d]
        qe, att, updates = prepared
        cc, rows = chunk_rows(c, reverse)
        for hh in range(2):
            h = 2 * hp + hh
            vs = slice(h * B_VAL_DIM, (h + 1) * B_VAL_DIM)
            vh = v_ref[rows, vs]
            state = s_ref[h]
            lhs = jnp.concatenate([qe[:, hh * B_KEY_DIM:(hh + 1) * B_KEY_DIM], att], axis=1)
            rhs = jnp.concatenate([state.astype(BF16)] + ([vh, zero_v] if hh == 0 else [zero_v, vh]), axis=0)
            o_ref[rows, vs] = jnp.dot(lhs, rhs, preferred_element_type=F32).astype(o_ref.dtype)
            decay = d_ref[h * B_KEY_DIM:(h + 1) * B_KEY_DIM, d_col0 + cc:d_col0 + cc + 1]
            s_ref[h] = state * decay + updates[hh]

    prepared = {}
    width = 2
    n_steps = len(units) // width
    for n in range(n_steps + 1):
        for u in range(width):
            if n < n_steps:
                prepared[n * width + u] = prepare(*units[n * width + u])
        for u in range(width):
            if n >= 1:
                i1 = (n - 1) * width + u
                apply(prepared.pop(i1), *units[i1])


def _gla_scan(v, decayed_f, decayed_b, tot, batch, seq_len):
    n = v.shape[0]
    ts = GLA_TILE
    nt = seq_len // ts
    n_tot = tot.shape[1]

    def fwd_tile(b, i):
        return b * nt + i

    def bwd_tile(b, i):
        return b * nt + nt - 1 - i

    def token_specs(tile_of):
        rows = lambda b, i: (tile_of(b, i), 0)
        return [pl.BlockSpec((ts, B_QK_COLS), rows)] * 3 + [
            pl.BlockSpec((ts, B_V_COLS), rows),
            pl.BlockSpec((1, n_tot, B_QK_COLS), lambda b, i: (tile_of(b, i), 0, 0))]

    return pl.pallas_call(
        _gla_scan_kernel,
        grid=(batch, nt),
        in_specs=token_specs(fwd_tile) + token_specs(bwd_tile),
        out_specs=[pl.BlockSpec((ts, B_V_COLS), lambda b, i: (fwd_tile(b, i), 0)),
                   pl.BlockSpec((ts, B_V_COLS), lambda b, i: (bwd_tile(b, i), 0))],
        out_shape=[jax.ShapeDtypeStruct((n, B_V_COLS), BF16), jax.ShapeDtypeStruct((n, B_V_COLS), BF16)],
        scratch_shapes=[
            pltpu.VMEM((B_QK_COLS, n_tot), F32),
            pltpu.VMEM((B_HEADS, B_KEY_DIM, B_VAL_DIM), F32),
            pltpu.VMEM((B_HEADS, B_KEY_DIM, B_VAL_DIM), F32),
        ],
        compiler_params=_compiler_params(("parallel", "arbitrary")),
        name="gla_scan",
    )(*decayed_f, v, tot, *decayed_b, v, tot)


def _row(v):
    return v.reshape(1, -1).astype(F32)


def _pad_gate_weight(w2, first_row):
    out = jnp.zeros((LANES, w2.shape[1]), BF16)
    return lax.dynamic_update_slice(out, w2.astype(BF16), (first_row, 0))


def kernel(x, positions, attn_w_in, attn_sink, attn_w_out, gla_w_in, gla_gate_w2_fwd, gla_gate_b_fwd,
           gla_gate_w2_bwd, gla_gate_b_bwd, gla_norm_g, gla_w_out, mix_ln_g, mix_ln_b, mlp_w1, mlp_w2,
           mlp_ln_g, mlp_ln_b):
    batch, seq_len, d_model = x.shape
    n = batch * seq_len
    x2 = x.reshape(n, d_model)

    inv_freq = ROPE_THETA ** (-jnp.arange(0, A_HEAD_DIM, 2, dtype=F32) / A_HEAD_DIM)
    ang_t = inv_freq[:, None] * positions.astype(F32).reshape(1, n)
    q_cols = A_HEADS * A_HEAD_DIM
    k_cols = A_KV_HEADS * A_HEAD_DIM

    for i in range(DEPTH):
        j = i // 2
        w1 = mlp_w1[i].astype(BF16)
        w2 = mlp_w2[i].astype(BF16)
        tail = (_row(mix_ln_g[i]), _row(mix_ln_b[i]), w1, w2, _row(mlp_ln_g[i]), _row(mlp_ln_b[i]))
        if i % 2 == 0:
            w_in = attn_w_in[j].astype(BF16)
            wqv_t = jnp.concatenate([w_in[:, :q_cols], w_in[:, q_cols + k_cols:]], axis=1).T
            qt, kd, vt = _attn_in_proj(x2, ang_t, wqv_t, w_in[:, q_cols:q_cols + k_cols])
            o = _window_attention(qt, kd, vt, attn_sink[j].astype(F32), batch, seq_len)
            x2 = _post_mixer(_attn_post_kernel, [o], [], x2, attn_w_out[j].astype(BF16), *tail)
        else:
            w_in = gla_w_in[j]
            lr_cols = 2 * B_GATE_RANK
            w_in = jnp.pad(w_in, ((0, 0), (0, LANES - lr_cols))).astype(BF16)
            v, r, *decayed, tot = _gla_in_proj(
                x2, w_in, _pad_gate_weight(gla_gate_w2_fwd[j], 0), _row(gla_gate_b_fwd[j]),
                _pad_gate_weight(gla_gate_w2_bwd[j], B_GATE_RANK), _row(gla_gate_b_bwd[j]))
            o_f, o_b = _gla_scan(v, decayed[:3], decayed[3:], tot, batch, seq_len)
            x2 = _post_mixer(_gla_post_kernel, [o_f, o_b, r], [_row(gla_norm_g[j])], x2,
                             gla_w_out[j].astype(BF16), *tail)
    return x2.reshape(batch, seq_len, d_model)
```

```python
import functools

import jax
import jax.numpy as jnp
from jax import lax
from jax.experimental import pallas as pl
from jax.experimental.pallas import tpu as pltpu

D_MODEL = 1024
DEPTH = 2

A_HEADS = 16
A_KV_HEADS = 4
A_HEAD_DIM = D_MODEL // A_HEADS
A_GROUP = A_HEADS // A_KV_HEADS
WINDOW = 128
ROPE_THETA = 10000.0

B_HEADS = 4
B_KEY_DIM = (D_MODEL // 2) // B_HEADS
B_VAL_DIM = D_MODEL // B_HEADS
B_GATE_RANK = 16
B_GATE_TAU = 16.0
B_CHUNK = 64
B_QK_COLS = B_HEADS * B_KEY_DIM
B_V_COLS = B_HEADS * B_VAL_DIM

D_FF = 4 * D_MODEL
DN_ALPHA = float((2 * DEPTH) ** 0.25)
LN_EPS = 1e-5
HEAD_NORM_EPS = 1e-6
LOG2_E = 1.4426950408889634

LANES = 128
BF16_ROWS = 16
V_ROWS = A_HEAD_DIM + BF16_ROWS
VMEM_LIMIT_BYTES = 56 * 1024 * 1024

TOKEN_TILE = 512
ATTN_Q_TILE = 2048
GLA_TILE = TOKEN_TILE
FF_CHUNK = 1024
POST_PART_ROWS = 256
ATTN_POST_TILE = 1024

BF16 = jnp.bfloat16
F32 = jnp.float32


def _const_spec(shape):
    zeros = (0,) * len(shape)
    return pl.BlockSpec(shape, lambda *_: zeros, pipeline_mode=pl.Buffered(1))


def _compiler_params(semantics, flags=None):
    return pltpu.CompilerParams(dimension_semantics=semantics, vmem_limit_bytes=VMEM_LIMIT_BYTES, flags=flags)


def _layer_norm(z, g, b):
    mu = jnp.mean(z, axis=-1, keepdims=True)
    zc = z - mu
    var = jnp.mean(zc * zc, axis=-1, keepdims=True)
    return zc * lax.rsqrt(var + LN_EPS) * g + b


def _lane_index(shape):
    return lax.broadcasted_iota(jnp.int32, shape, len(shape) - 1)


def _attn_in_kernel(x_ref, ang_ref, wqv_ref, wk_ref, qt_ref, k_ref, vt_ref):
    xb = x_ref[...].astype(BF16)
    half = A_HEAD_DIM // 2
    q_cols = A_HEADS * A_HEAD_DIM
    kv_cols = A_KV_HEADS * A_HEAD_DIM
    scale = A_HEAD_DIM ** -0.5 * LOG2_E
    block = 4 * A_HEAD_DIM

    def project_t(lo):
        return lax.dot_general(wqv_ref[lo:lo + block, :], xb, (((1,), (1,)), ((), ())),
                               preferred_element_type=F32)

    def rotate_q(hq, lo):
        for h in range(block // A_HEAD_DIM):
            r0 = h * A_HEAD_DIM
            t1 = hq[r0:r0 + half, :]
            t2 = hq[r0 + half:r0 + A_HEAD_DIM, :]
            qt_ref[lo + r0:lo + r0 + half, :] = ((t1 * cos_t - t2 * sin_t) * scale).astype(BF16)
            qt_ref[lo + r0 + half:lo + r0 + A_HEAD_DIM, :] = ((t2 * cos_t + t1 * sin_t) * scale).astype(BF16)

    def rotate_k(hk):
        cos = jnp.concatenate([cos_t] * (LANES // half), axis=0).T
        sin = jnp.concatenate([sin_t] * (LANES // half), axis=0).T
        lane = _lane_index(cos.shape)
        first_half = (lane % A_HEAD_DIM) < half
        sin_signed = jnp.where(first_half, -sin, sin)
        low_head = lane < A_HEAD_DIM
        for j in range(kv_cols // LANES):
            t = hk[:, j * LANES:(j + 1) * LANES]
            partner = jnp.where(first_half, pltpu.roll(t, LANES - half, 1), pltpu.roll(t, half, 1))
            t = t * cos + partner * sin_signed
            swapped = pltpu.roll(t, A_HEAD_DIM, 1)
            k_ref[:, (2 * j) * LANES:(2 * j + 1) * LANES] = jnp.where(low_head, t, swapped).astype(BF16)
            k_ref[:, (2 * j + 1) * LANES:(2 * j + 2) * LANES] = jnp.where(low_head, swapped, t).astype(BF16)

    hk = jnp.dot(xb, wk_ref[...], preferred_element_type=F32)
    cos_t = jnp.cos(ang_ref[...])
    sin_t = jnp.sin(ang_ref[...])
    pieces = {0: project_t(0)}
    rotate_k(hk)
    for p in range(1, q_cols // block):
        pieces[p] = project_t(p * block)
        rotate_q(pieces.pop(p - 1), (p - 1) * block)
    hv = project_t(q_cols)
    rotate_q(pieces.pop(q_cols // block - 1), q_cols - block)
    vt_ref[...] = hv.astype(BF16)


def _attn_in_proj(x2, ang_t, wqv_t, wk):
    n = x2.shape[0]
    tm = TOKEN_TILE
    q_cols = A_HEADS * A_HEAD_DIM
    kv_cols = A_KV_HEADS * A_HEAD_DIM
    k_dup = A_KV_HEADS * LANES
    return pl.pallas_call(
        _attn_in_kernel,
        grid=(n // tm,),
        in_specs=[
            pl.BlockSpec((tm, D_MODEL), lambda i: (i, 0)),
            pl.BlockSpec((ang_t.shape[0], tm), lambda i: (0, i)),
            _const_spec(wqv_t.shape),
            _const_spec(wk.shape),
        ],
        out_specs=[
            pl.BlockSpec((q_cols, tm), lambda i: (0, i)),
            pl.BlockSpec((tm, k_dup), lambda i: (i, 0)),
            pl.BlockSpec((kv_cols, tm), lambda i: (0, i)),
        ],
        out_shape=[
            jax.ShapeDtypeStruct((q_cols, n), BF16),
            jax.ShapeDtypeStruct((n, k_dup), BF16),
            jax.ShapeDtypeStruct((kv_cols, n), BF16),
        ],
        compiler_params=_compiler_params(("parallel",)),
        name="attn_in_proj",
    )(x2, ang_t, wqv_t, wk)


def _window_attn_kernel(sink_ref, qt_ref, kp_ref, km_ref, kn_ref, vp_ref, vm_ref, vn_ref, o_ref,
                        kbuf, vbuf):
    tq = ATTN_Q_TILE
    blk = WINDOW
    span = 3 * blk
    i = pl.program_id(1)
    last_tile = pl.num_programs(1) - 1
    kbuf[0:blk, :] = kp_ref[...]
    kbuf[blk:blk + tq, :] = km_ref[...]
    kbuf[blk + tq:, :] = kn_ref[...]
    for g in range(A_KV_HEADS):
        src = slice(g * A_HEAD_DIM, (g + 1) * A_HEAD_DIM)
        dst = slice(g * V_ROWS, g * V_ROWS + A_HEAD_DIM)
        vbuf[dst, 0:blk] = vp_ref[src, :]
        vbuf[dst, blk:blk + tq] = vm_ref[src, :]
        vbuf[dst, blk + tq:] = vn_ref[src, :]
        vbuf[g * V_ROWS + A_HEAD_DIM:(g + 1) * V_ROWS, :] = jnp.ones((BF16_ROWS, tq + 2 * blk), BF16)

    pair_shape = (blk, 2 * LANES)
    key = lax.broadcasted_iota(jnp.int32, pair_shape, 0)
    query = _lane_index(pair_shape) % blk
    neg_inf = jnp.float32(-jnp.inf)
    prev_bias = jnp.where(key >= query, 0.0, neg_inf)
    next_bias = jnp.where(key <= query, 0.0, neg_inf)
    first_prev_bias = jnp.where(i == 0, neg_inf, prev_bias)
    last_next_bias = jnp.where(i == last_tile, neg_inf, next_bias)
    low_rows = lax.broadcasted_iota(jnp.int32, (LANES, blk), 0) < A_HEAD_DIM
    zero = jnp.zeros((LANES, blk), BF16)

    n_blocks = tq // blk
    units = [(jj, g, pr) for jj in range(n_blocks) for g in range(A_KV_HEADS) for pr in range(A_GROUP // 2)]

    def pair_lanes(g, pr):
        return slice((g * 2 + pr) * LANES, (g * 2 + pr + 1) * LANES)

    def scores(jj, g, pr):
        kk = kbuf[jj * blk:jj * blk + span, g * LANES:(g + 1) * LANES]
        qt = qt_ref[pair_lanes(g, pr), jj * blk:(jj + 1) * blk]
        w = jnp.concatenate([jnp.where(low_rows, qt, zero), jnp.where(low_rows, zero, qt)], axis=1)
        return jnp.dot(kk, w, preferred_element_type=F32)

    def softmax(s, jj, g, pr):
        s_prev = s[0:blk] + (first_prev_bias if jj == 0 else prev_bias)
        s_mid = s[blk:2 * blk]
        s_next = s[2 * blk:] + (last_next_bias if jj == n_blocks - 1 else next_bias)
        head = g * A_GROUP + 2 * pr
        sink = jnp.concatenate([jnp.full((1, blk), sink_ref[head] * LOG2_E, F32),
                                jnp.full((1, blk), sink_ref[head + 1] * LOG2_E, F32)], axis=1)
        m = jnp.maximum(jnp.maximum(jnp.max(s_prev, axis=0, keepdims=True),
                                    jnp.max(s_mid, axis=0, keepdims=True)),
                        jnp.maximum(jnp.max(s_next, axis=0, keepdims=True), sink))
        p = jnp.concatenate([jnp.exp2(s_prev - m), jnp.exp2(s_mid - m), jnp.exp2(s_next - m)], axis=0)
        return p.astype(BF16), jnp.exp2(sink - m)

    def weighted_values(p, sink_term, jj, g, pr):
        vt = vbuf[g * V_ROWS:(g + 1) * V_ROWS, jj * blk:jj * blk + span]
        ot = jnp.dot(vt, p, preferred_element_type=F32)
        denom = ot[A_HEAD_DIM:A_HEAD_DIM + 1, :] + sink_term
        ot = ot[0:A_HEAD_DIM, :] * (1.0 / denom)
        both = jnp.concatenate([ot[:, 0:blk], ot[:, blk:]], axis=0)
        o_ref[jj * blk:(jj + 1) * blk, pair_lanes(g, pr)] = both.T.astype(BF16)

    s_stage = {}
    p_stage = {}
    width = 4
    n_steps = len(units) // width
    for n in range(n_steps + 2):
        for u in range(width):
            if n < n_steps:
                s_stage[n * width + u] = scores(*units[n * width + u])
        for u in range(width):
            if 0 <= n - 1 < n_steps:
                i1 = (n - 1) * width + u
                p_stage[i1] = softmax(s_stage.pop(i1), *units[i1])
        for u in range(width):
            if 0 <= n - 2 < n_steps:
                i2 = (n - 2) * width + u
                weighted_values(*p_stage.pop(i2), *units[i2])


def _window_attention(qt, kd, vt, sink, batch, seq_len):
    q_cols, n = qt.shape
    v_cols = vt.shape[0]
    tq = ATTN_Q_TILE
    blk = WINDOW
    nt = seq_len // tq
    per_tile = tq // blk
    nblk = seq_len // blk
    k_dup = kd.shape[1]

    def main_idx(b, i):
        return b * nt + i

    def prev_idx(b, i):
        return b * nblk + jnp.maximum(i * per_tile - 1, 0)

    def next_idx(b, i):
        return b * nblk + jnp.minimum((i + 1) * per_tile, nblk - 1)

    k_specs = [pl.BlockSpec((blk, k_dup), lambda b, i: (prev_idx(b, i), 0)),
               pl.BlockSpec((tq, k_dup), lambda b, i: (main_idx(b, i), 0)),
               pl.BlockSpec((blk, k_dup), lambda b, i: (next_idx(b, i), 0))]
    v_specs = [pl.BlockSpec((v_cols, blk), lambda b, i: (0, prev_idx(b, i))),
               pl.BlockSpec((v_cols, tq), lambda b, i: (0, main_idx(b, i))),
               pl.BlockSpec((v_cols, blk), lambda b, i: (0, next_idx(b, i)))]
    return pl.pallas_call(
        _window_attn_kernel,
        grid=(batch, nt),
        in_specs=[pl.BlockSpec(memory_space=pltpu.SMEM),
                  pl.BlockSpec((q_cols, tq), lambda b, i: (0, main_idx(b, i)))] + k_specs + v_specs,
        out_specs=pl.BlockSpec((tq, q_cols), lambda b, i: (main_idx(b, i), 0)),
        out_shape=jax.ShapeDtypeStruct((n, q_cols), BF16),
        scratch_shapes=[pltpu.VMEM((tq + 2 * blk, k_dup), BF16),
                        pltpu.VMEM((A_KV_HEADS * V_ROWS, tq + 2 * blk), BF16)],
        compiler_params=_compiler_params(("parallel", "parallel")),
        name="window_attn",
    )(sink, qt, kd, kd, kd, vt, vt, vt)


def _residual_mlp_tail(mixer_out, x_ref, wo_ref, ln1g_ref, ln1b_ref, w1_ref, w2_ref, ln2g_ref, ln2b_ref,
                       out_ref):
    tm = x_ref.shape[0]
    parts = [slice(lo, lo + POST_PART_ROWS) for lo in range(0, tm, POST_PART_ROWS)]
    n_ff = D_FF // FF_CHUNK

    def project(rows):
        return jnp.dot(mixer_out(rows), wo_ref[...], preferred_element_type=F32)

    def norm1(y, rows):
        x1 = _layer_norm(DN_ALPHA * x_ref[rows, :] + y, ln1g_ref[...], ln1b_ref[...])
        return x1, x1.astype(BF16)

    def mlp_chunk(x1b, acc, c):
        hc = jnp.dot(x1b, w1_ref[:, c * FF_CHUNK:(c + 1) * FF_CHUNK], preferred_element_type=F32)
        hc = jnp.square(jnp.maximum(hc, 0.0)).astype(BF16)
        part = jnp.dot(hc, w2_ref[c * FF_CHUNK:(c + 1) * FF_CHUNK, :], preferred_element_type=F32)
        return part if acc is None else acc + part

    def norm2(x1, acc, rows):
        out_ref[rows, :] = _layer_norm(DN_ALPHA * x1 + acc, ln2g_ref[...], ln2b_ref[...])

    n_parts = len(parts)
    normed = {0: norm1(project(parts[0]), parts[0])}
    done = None
    for k in range(n_parts):
        x1, x1b = normed.pop(k)
        if k + 1 < n_parts:
            y_next = project(parts[k + 1])
        acc = mlp_chunk(x1b, None, 0)
        if done is not None:
            norm2(*done)
        acc = mlp_chunk(x1b, acc, 1)
        if k + 1 < n_parts:
            normed[k + 1] = norm1(y_next, parts[k + 1])
        for c in range(2, n_ff):
            acc = mlp_chunk(x1b, acc, c)
        done = (x1, acc, parts[k])
    norm2(*done)


def _attn_post_kernel(o_ref, *tail_refs):
    _residual_mlp_tail(lambda rows: o_ref[rows, :], *tail_refs)


def _gla_post_kernel(of_ref, ob_ref, r_ref, ng_ref, *tail_refs):
    def mixer_out(rows):
        o = of_ref[rows, :].astype(F32) + ob_ref[rows, :].astype(F32)
        r = r_ref[rows, :]
        gate = r * (1.0 / (1.0 + jnp.exp(-r)))
        ng = ng_ref[...]
        heads = []
        for h in range(B_HEADS):
            oh = o[:, h * B_VAL_DIM:(h + 1) * B_VAL_DIM]
            oh = oh * lax.rsqrt(jnp.mean(oh * oh, axis=-1, keepdims=True) + HEAD_NORM_EPS)
            heads.append((oh * ng * gate[:, h * B_VAL_DIM:(h + 1) * B_VAL_DIM]).astype(BF16))
        return jnp.concatenate(heads, axis=-1)

    _residual_mlp_tail(mixer_out, *tail_refs)


def _post_mixer(kernel_fn, tm, mixer_inputs, extra_consts, x2, wo, ln1g, ln1b, w1, w2, ln2g, ln2b):
    n = x2.shape[0]
    tile = lambda cols: pl.BlockSpec((tm, cols), lambda i: (i, 0))
    consts = list(extra_consts)
    return pl.pallas_call(
        kernel_fn,
        grid=(n // tm,),
        in_specs=[tile(a.shape[1]) for a in mixer_inputs]
        + [_const_spec(c.shape) for c in consts]
        + [tile(D_MODEL), _const_spec(wo.shape), _const_spec(ln1g.shape), _const_spec(ln1b.shape),
           _const_spec(w1.shape), _const_spec(w2.shape), _const_spec(ln2g.shape), _const_spec(ln2b.shape)],
        out_specs=tile(D_MODEL),
        out_shape=jax.ShapeDtypeStruct((n, D_MODEL), F32),
        compiler_params=_compiler_params(("parallel",)),
        name=kernel_fn.__name__.strip("_"),
    )(*mixer_inputs, *consts, x2, wo, ln1g, ln1b, w1, w2, ln2g, ln2b)


def _gla_in_kernel(x_ref, w_ref, w2f_ref, gbf_ref, w2b_ref, gbb_ref,
                   v_ref, r_ref, qef_ref, kef_ref, kdf_ref, qeb_ref, keb_ref, kdb_ref, tot_ref):
    xb = x_ref[...].astype(BF16)
    tm = x_ref.shape[0]
    c1 = B_QK_COLS
    c2 = 2 * B_QK_COLS
    c3 = c2 + B_V_COLS
    c4 = c3 + B_V_COLS
    n_chunks = tm // B_CHUNK

    def project(lo, hi):
        return jnp.dot(xb, w_ref[:, lo:hi], preferred_element_type=F32)

    def chunk_totals(b, reverse):
        edge = 0 if reverse else B_CHUNK - 1
        return [b[c * B_CHUNK + edge:c * B_CHUNK + edge + 1, :] for c in range(n_chunks)]

    def decayed(q, k, b, totals, qe_ref, ke_ref, kd_ref):
        decay_total = jnp.concatenate(
            [jnp.broadcast_to(jnp.exp2(t), (B_CHUNK, t.shape[1])) for t in totals], axis=0)
        ke = k * jnp.exp2(-b)
        qe_ref[...] = (q * jnp.exp2(b)).astype(BF16)
        ke_ref[...] = ke.astype(BF16)
        kd_ref[...] = (ke * decay_total).astype(BF16)

    lr = project(c4, c4 + LANES).astype(BF16)
    q = project(0, c1) * (B_KEY_DIM ** -0.5)
    g_f = _log_gate(lr, w2f_ref[...], gbf_ref[...])
    k = project(c1, c2)
    g_b = _log_gate(lr, w2b_ref[...], gbb_ref[...])
    v_ref[...] = project(c2, c3).astype(BF16)
    b_f = _chunk_cumsum(g_f, False)
    totals_f = chunk_totals(b_f, False)
    decayed(q, k, b_f, totals_f, qef_ref, kef_ref, kdf_ref)
    b_b = _chunk_cumsum(g_b, True)
    r_ref[...] = project(c3, c4)
    totals_b = chunk_totals(b_b, True)
    decayed(q, k, b_b, totals_b, qeb_ref, keb_ref, kdb_ref)
    tot_ref[0] = jnp.concatenate(totals_f + totals_b, axis=0)


def _gla_in_proj(x2, w, w2f, gbf, w2b, gbb):
    n = x2.shape[0]
    tm = TOKEN_TILE
    n_tot = 2 * (tm // B_CHUNK)
    tile = lambda cols: pl.BlockSpec((tm, cols), lambda i: (i, 0))
    qk = jax.ShapeDtypeStruct((n, B_QK_COLS), BF16)
    return pl.pallas_call(
        _gla_in_kernel,
        grid=(n // tm,),
        in_specs=[tile(D_MODEL), _const_spec(w.shape), _const_spec(w2f.shape), _const_spec(gbf.shape),
                  _const_spec(w2b.shape), _const_spec(gbb.shape)],
        out_specs=[tile(B_V_COLS), tile(B_V_COLS)] + [tile(B_QK_COLS)] * 6
        + [pl.BlockSpec((1, n_tot, B_QK_COLS), lambda i: (i, 0, 0))],
        out_shape=[jax.ShapeDtypeStruct((n, B_V_COLS), BF16), jax.ShapeDtypeStruct((n, B_V_COLS), F32)]
        + [qk] * 6 + [jax.ShapeDtypeStruct((n // tm, n_tot, B_QK_COLS), F32)],
        compiler_params=_compiler_params(("parallel",)),
        name="gla_in_proj",
    )(x2, w, w2f, gbf, w2b, gbb)


def _log_gate(lr, w2, bias):
    z = jnp.dot(lr, w2, preferred_element_type=F32) + bias
    log_term = jnp.log2(1.0 + jnp.exp(-jnp.abs(z)))
    return jnp.minimum(z, 0.0) * (LOG2_E / B_GATE_TAU) - log_term * (1.0 / B_GATE_TAU)


def _chunk_cumsum(g, reverse):
    hi = g.astype(BF16)
    rest = g - hi.astype(F32)
    mid = rest.astype(BF16)
    lo = (rest - mid.astype(F32)).astype(BF16)
    t_row = lax.broadcasted_iota(jnp.int32, (B_CHUNK, 3 * B_CHUNK), 0)
    t_col = lax.broadcasted_iota(jnp.int32, (B_CHUNK, 3 * B_CHUNK), 1) % B_CHUNK
    keep = (t_row <= t_col) if reverse else (t_row >= t_col)
    tri = jnp.where(keep, 1.0, 0.0).astype(BF16)
    out = []
    for c in range(g.shape[0] // B_CHUNK):
        rows = slice(c * B_CHUNK, (c + 1) * B_CHUNK)
        terms = jnp.concatenate([hi[rows], mid[rows], lo[rows]], axis=0)
        out.append(jnp.dot(tri, terms, preferred_element_type=F32))
    return jnp.concatenate(out, axis=0)


def _gla_scan_kernel(qef_ref, kef_ref, kdf_ref, vf_ref, totf_ref, qeb_ref, keb_ref, kdb_ref, vb_ref, totb_ref,
                     of_ref, ob_ref, d_ref, sf_ref, sb_ref):
    @pl.when(pl.program_id(1) == 0)
    def _():
        sf_ref[...] = jnp.zeros_like(sf_ref)
        sb_ref[...] = jnp.zeros_like(sb_ref)

    n_chunks = GLA_TILE // B_CHUNK
    pair_k = 2 * B_KEY_DIM
    totals = jnp.concatenate([totf_ref[0, 0:n_chunks, :], totb_ref[0, n_chunks:, :]], axis=0)
    d_ref[...] = jnp.exp2(totals).T

    directions = ((False, qef_ref, kef_ref, kdf_ref, vf_ref, 0, of_ref, sf_ref),
                  (True, qeb_ref, keb_ref, kdb_ref, vb_ref, n_chunks, ob_ref, sb_ref))
    t_row = lax.broadcasted_iota(jnp.int32, (B_CHUNK, 2 * B_CHUNK), 0)
    t_col = _lane_index((B_CHUNK, 2 * B_CHUNK)) % B_CHUNK
    first_head_lanes = _lane_index((B_CHUNK, pair_k)) < B_KEY_DIM
    zero_k = jnp.zeros((B_CHUNK, pair_k), BF16)
    zero_v = jnp.zeros((B_CHUNK, B_VAL_DIM), BF16)
    units = [(c, d, hp) for c in range(n_chunks) for d in range(2) for hp in range(B_HEADS // 2)]

    def chunk_rows(c, reverse):
        cc = n_chunks - 1 - c if reverse else c
        return cc, slice(cc * B_CHUNK, (cc + 1) * B_CHUNK)

    def prepare(c, d, hp):
        reverse, qe_ref, ke_ref, kd_ref, v_ref = directions[d][:5]
        _, rows = chunk_rows(c, reverse)
        cols = slice(hp * pair_k, (hp + 1) * pair_k)
        qe = qe_ref[rows, cols]
        ke = ke_ref[rows, cols]
        kd = kd_ref[rows, cols]
        ke_bd = jnp.concatenate([jnp.where(first_head_lanes, ke, zero_k),
                                 jnp.where(first_head_lanes, zero_k, ke)], axis=0)
        att = lax.dot_general(qe, ke_bd, (((1,), (1,)), ((), ())), preferred_element_type=F32)
        causal = (t_row <= t_col) if reverse else (t_row >= t_col)
        att = jnp.where(causal, att, 0.0).astype(BF16)
        updates = []
        for hh in range(2):
            h = 2 * hp + hh
            vh = v_ref[rows, h * B_VAL_DIM:(h + 1) * B_VAL_DIM]
            updates.append(lax.dot_general(kd[:, hh * B_KEY_DIM:(hh + 1) * B_KEY_DIM], vh,
                                           (((0,), (0,)), ((), ())), preferred_element_type=F32))
        return qe, att, updates

    def apply(prepared, c, d, hp):
        reverse, _, _, _, v_ref, d_col0, o_ref, s_ref = directions[d]
        qe, att, updates = prepared
        cc, rows = chunk_rows(c, reverse)
        for hh in range(2):
            h = 2 * hp + hh
            vs = slice(h * B_VAL_DIM, (h + 1) * B_VAL_DIM)
            vh = v_ref[rows, vs]
            state = s_ref[h]
            lhs = jnp.concatenate([qe[:, hh * B_KEY_DIM:(hh + 1) * B_KEY_DIM], att], axis=1)
            rhs = jnp.concatenate([state.astype(BF16)] + ([vh, zero_v] if hh == 0 else [zero_v, vh]), axis=0)
            o_ref[rows, vs] = jnp.dot(lhs, rhs, preferred_element_type=F32).astype(o_ref.dtype)
            decay = d_ref[h * B_KEY_DIM:(h + 1) * B_KEY_DIM, d_col0 + cc:d_col0 + cc + 1]
            s_ref[h] = state * decay + updates[hh]

    prepared = {}
    width = 2
    n_steps = len(units) // width
    for n in range(n_steps + 1):
        for u in range(width):
            if n < n_steps:
                prepared[n * width + u] = prepare(*units[n * width + u])
        for u in range(width):
            if n >= 1:
                i1 = (n - 1) * width + u
                apply(prepared.pop(i1), *units[i1])


def _gla_scan(v, decayed_f, decayed_b, tot, batch, seq_len):
    n = v.shape[0]
    ts = GLA_TILE
    nt = seq_len // ts
    n_tot = tot.shape[1]

    def fwd_tile(b, i):
        return b * nt + i

    def bwd_tile(b, i):
        return b * nt + nt - 1 - i

    def token_specs(tile_of):
        rows = lambda b, i: (tile_of(b, i), 0)
        return [pl.BlockSpec((ts, B_QK_COLS), rows)] * 3 + [
            pl.BlockSpec((ts, B_V_COLS), rows),
            pl.BlockSpec((1, n_tot, B_QK_COLS), lambda b, i: (tile_of(b, i), 0, 0))]

    return pl.pallas_call(
        _gla_scan_kernel,
        grid=(batch, nt),
        in_specs=token_specs(fwd_tile) + token_specs(bwd_tile),
        out_specs=[pl.BlockSpec((ts, B_V_COLS), lambda b, i: (fwd_tile(b, i), 0)),
                   pl.BlockSpec((ts, B_V_COLS), lambda b, i: (bwd_tile(b, i), 0))],
        out_shape=[jax.ShapeDtypeStruct((n, B_V_COLS), BF16), jax.ShapeDtypeStruct((n, B_V_COLS), BF16)],
        scratch_shapes=[
            pltpu.VMEM((B_QK_COLS, n_tot), F32),
            pltpu.VMEM((B_HEADS, B_KEY_DIM, B_VAL_DIM), F32),
            pltpu.VMEM((B_HEADS, B_KEY_DIM, B_VAL_DIM), F32),
        ],
        compiler_params=_compiler_params(("parallel", "arbitrary")),
        name="gla_scan",
    )(*decayed_f, v, tot, *decayed_b, v, tot)


def _row(v):
    return v.reshape(1, -1).astype(F32)


def _pad_gate_weight(w2, first_row):
    out = jnp.zeros((LANES, w2.shape[1]), BF16)
    return lax.dynamic_update_slice(out, w2.astype(BF16), (first_row, 0))


def kernel(x, positions, attn_w_in, attn_sink, attn_w_out, gla_w_in, gla_gate_w2_fwd, gla_gate_b_fwd,
           gla_gate_w2_bwd, gla_gate_b_bwd, gla_norm_g, gla_w_out, mix_ln_g, mix_ln_b, mlp_w1, mlp_w2,
           mlp_ln_g, mlp_ln_b):
    batch, seq_len, d_model = x.shape
    n = batch * seq_len
    assert d_model == D_MODEL
    assert seq_len % ATTN_Q_TILE == 0 and seq_len % GLA_TILE == 0 and n % ATTN_POST_TILE == 0
    x2 = x.reshape(n, d_model)

    inv_freq = ROPE_THETA ** (-jnp.arange(0, A_HEAD_DIM, 2, dtype=F32) / A_HEAD_DIM)
    ang_t = inv_freq[:, None] * positions.astype(F32).reshape(1, n)
    q_cols = A_HEADS * A_HEAD_DIM
    k_cols = A_KV_HEADS * A_HEAD_DIM

    for i in range(DEPTH):
        j = i // 2
        w1 = mlp_w1[i].astype(BF16)
        w2 = mlp_w2[i].astype(BF16)
        tail = (_row(mix_ln_g[i]), _row(mix_ln_b[i]), w1, w2, _row(mlp_ln_g[i]), _row(mlp_ln_b[i]))
        if i % 2 == 0:
            w_in = attn_w_in[j].astype(BF16)
            wqv_t = jnp.concatenate([w_in[:, :q_cols], w_in[:, q_cols + k_cols:]], axis=1).T
            qt, kd, vt = _attn_in_proj(x2, ang_t, wqv_t, w_in[:, q_cols:q_cols + k_cols])
            o = _window_attention(qt, kd, vt, attn_sink[j].astype(F32), batch, seq_len)
            x2 = _post_mixer(_attn_post_kernel, ATTN_POST_TILE, [o], [], x2, attn_w_out[j].astype(BF16), *tail)
        else:
            w_in = gla_w_in[j]
            lr_cols = 2 * B_GATE_RANK
            w_in = jnp.pad(w_in, ((0, 0), (0, LANES - lr_cols))).astype(BF16)
            v, r, *decayed, tot = _gla_in_proj(
                x2, w_in, _pad_gate_weight(gla_gate_w2_fwd[j], 0), _row(gla_gate_b_fwd[j]),
                _pad_gate_weight(gla_gate_w2_bwd[j], B_GATE_RANK), _row(gla_gate_b_bwd[j]))
            o_f, o_b = _gla_scan(v, decayed[:3], decayed[3:], tot, batch, seq_len)
            x2 = _post_mixer(_gla_post_kernel, TOKEN_TILE, [o_f, o_b, r], [_row(gla_norm_g[j])], x2,
                             gla_w_out[j].astype(BF16), *tail)
    return x2.reshape(batch, seq_len, d_model)
```

```python
import functools

import jax
import jax.numpy as jnp
from jax import lax
from jax.experimental import pallas as pl
from jax.experimental.pallas import tpu as pltpu

D_MODEL = 1024
DEPTH = 2

A_HEADS = 16
A_KV_HEADS = 4
A_HEAD_DIM = D_MODEL // A_HEADS
A_GROUP = A_HEADS // A_KV_HEADS
WINDOW = 128
ROPE_THETA = 10000.0

B_HEADS = 4
B_KEY_DIM = (D_MODEL // 2) // B_HEADS
B_VAL_DIM = D_MODEL // B_HEADS
B_GATE_RANK = 16
B_GATE_TAU = 16.0
B_CHUNK = 64
B_QK_COLS = B_HEADS * B_KEY_DIM
B_V_COLS = B_HEADS * B_VAL_DIM

D_FF = 4 * D_MODEL
DN_ALPHA = float((2 * DEPTH) ** 0.25)
LN_EPS = 1e-5
HEAD_NORM_EPS = 1e-6
LOG2_E = 1.4426950408889634

LANES = 128
BF16_ROWS = 16
V_ROWS = A_HEAD_DIM + BF16_ROWS
VMEM_LIMIT_BYTES = 56 * 1024 * 1024

TOKEN_TILE = 512
ATTN_Q_TILE = 2048
GLA_TILE = 2 * TOKEN_TILE
FF_CHUNK = 1024
POST_PART_ROWS = 256
ATTN_POST_TILE = TOKEN_TILE

BF16 = jnp.bfloat16
F32 = jnp.float32


def _const_spec(shape):
    zeros = (0,) * len(shape)
    return pl.BlockSpec(shape, lambda *_: zeros, pipeline_mode=pl.Buffered(1))


def _compiler_params(semantics, flags=None):
    return pltpu.CompilerParams(dimension_semantics=semantics, vmem_limit_bytes=VMEM_LIMIT_BYTES, flags=flags)


def _layer_norm(z, g, b):
    mu = jnp.mean(z, axis=-1, keepdims=True)
    zc = z - mu
    var = jnp.mean(zc * zc, axis=-1, keepdims=True)
    return zc * lax.rsqrt(var + LN_EPS) * g + b


def _lane_index(shape):
    return lax.broadcasted_iota(jnp.int32, shape, len(shape) - 1)


def _attn_in_kernel(x_ref, ang_ref, wqv_ref, wk_ref, qt_ref, k_ref, vt_ref):
    xb = x_ref[...].astype(BF16)
    half = A_HEAD_DIM // 2
    q_cols = A_HEADS * A_HEAD_DIM
    kv_cols = A_KV_HEADS * A_HEAD_DIM
    scale = A_HEAD_DIM ** -0.5 * LOG2_E
    block = 4 * A_HEAD_DIM

    def project_t(lo):
        return lax.dot_general(wqv_ref[lo:lo + block, :], xb, (((1,), (1,)), ((), ())),
                               preferred_element_type=F32)

    def rotate_q(hq, lo):
        for h in range(block // A_HEAD_DIM):
            r0 = h * A_HEAD_DIM
            t1 = hq[r0:r0 + half, :]
            t2 = hq[r0 + half:r0 + A_HEAD_DIM, :]
            qt_ref[lo + r0:lo + r0 + half, :] = ((t1 * cos_t - t2 * sin_t) * scale).astype(BF16)
            qt_ref[lo + r0 + half:lo + r0 + A_HEAD_DIM, :] = ((t2 * cos_t + t1 * sin_t) * scale).astype(BF16)

    def rotate_k(hk):
        cos = jnp.concatenate([cos_t] * (LANES // half), axis=0).T
        sin = jnp.concatenate([sin_t] * (LANES // half), axis=0).T
        lane = _lane_index(cos.shape)
        first_half = (lane % A_HEAD_DIM) < half
        sin_signed = jnp.where(first_half, -sin, sin)
        low_head = lane < A_HEAD_DIM
        for j in range(kv_cols // LANES):
            t = hk[:, j * LANES:(j + 1) * LANES]
            partner = jnp.where(first_half, pltpu.roll(t, LANES - half, 1), pltpu.roll(t, half, 1))
            t = t * cos + partner * sin_signed
            swapped = pltpu.roll(t, A_HEAD_DIM, 1)
            k_ref[:, (2 * j) * LANES:(2 * j + 1) * LANES] = jnp.where(low_head, t, swapped).astype(BF16)
            k_ref[:, (2 * j + 1) * LANES:(2 * j + 2) * LANES] = jnp.where(low_head, swapped, t).astype(BF16)

    hk = jnp.dot(xb, wk_ref[...], preferred_element_type=F32)
    cos_t = jnp.cos(ang_ref[...])
    sin_t = jnp.sin(ang_ref[...])
    pieces = {0: project_t(0)}
    rotate_k(hk)
    for p in range(1, q_cols // block):
        pieces[p] = project_t(p * block)
        rotate_q(pieces.pop(p - 1), (p - 1) * block)
    hv = project_t(q_cols)
    rotate_q(pieces.pop(q_cols // block - 1), q_cols - block)
    vt_ref[...] = hv.astype(BF16)


def _attn_in_proj(x2, ang_t, wqv_t, wk):
    n = x2.shape[0]
    tm = TOKEN_TILE
    q_cols = A_HEADS * A_HEAD_DIM
    kv_cols = A_KV_HEADS * A_HEAD_DIM
    k_dup = A_KV_HEADS * LANES
    return pl.pallas_call(
        _attn_in_kernel,
        grid=(n // tm,),
        in_specs=[
            pl.BlockSpec((tm, D_MODEL), lambda i: (i, 0)),
            pl.BlockSpec((ang_t.shape[0], tm), lambda i: (0, i)),
            _const_spec(wqv_t.shape),
            _const_spec(wk.shape),
        ],
        out_specs=[
            pl.BlockSpec((q_cols, tm), lambda i: (0, i)),
            pl.BlockSpec((tm, k_dup), lambda i: (i, 0)),
            pl.BlockSpec((kv_cols, tm), lambda i: (0, i)),
        ],
        out_shape=[
            jax.ShapeDtypeStruct((q_cols, n), BF16),
            jax.ShapeDtypeStruct((n, k_dup), BF16),
            jax.ShapeDtypeStruct((kv_cols, n), BF16),
        ],
        compiler_params=_compiler_params(("parallel",)),
        name="attn_in_proj",
    )(x2, ang_t, wqv_t, wk)


def _window_attn_kernel(sink_ref, qt_ref, kp_ref, km_ref, kn_ref, vp_ref, vm_ref, vn_ref, o_ref,
                        kbuf, vbuf):
    tq = ATTN_Q_TILE
    blk = WINDOW
    span = 3 * blk
    i = pl.program_id(1)
    last_tile = pl.num_programs(1) - 1
    kbuf[0:blk, :] = kp_ref[...]
    kbuf[blk:blk + tq, :] = km_ref[...]
    kbuf[blk + tq:, :] = kn_ref[...]
    for g in range(A_KV_HEADS):
        src = slice(g * A_HEAD_DIM, (g + 1) * A_HEAD_DIM)
        dst = slice(g * V_ROWS, g * V_ROWS + A_HEAD_DIM)
        vbuf[dst, 0:blk] = vp_ref[src, :]
        vbuf[dst, blk:blk + tq] = vm_ref[src, :]
        vbuf[dst, blk + tq:] = vn_ref[src, :]
        vbuf[g * V_ROWS + A_HEAD_DIM:(g + 1) * V_ROWS, :] = jnp.ones((BF16_ROWS, tq + 2 * blk), BF16)

    pair_shape = (blk, 2 * LANES)
    key = lax.broadcasted_iota(jnp.int32, pair_shape, 0)
    query = _lane_index(pair_shape) % blk
    neg_inf = jnp.float32(-jnp.inf)
    prev_bias = jnp.where(key >= query, 0.0, neg_inf)
    next_bias = jnp.where(key <= query, 0.0, neg_inf)
    first_prev_bias = jnp.where(i == 0, neg_inf, prev_bias)
    last_next_bias = jnp.where(i == last_tile, neg_inf, next_bias)
    low_rows = lax.broadcasted_iota(jnp.int32, (LANES, blk), 0) < A_HEAD_DIM
    zero = jnp.zeros((LANES, blk), BF16)

    n_blocks = tq // blk
    units = [(jj, g, pr) for jj in range(n_blocks) for g in range(A_KV_HEADS) for pr in range(A_GROUP // 2)]

    def pair_lanes(g, pr):
        return slice((g * 2 + pr) * LANES, (g * 2 + pr + 1) * LANES)

    def scores(jj, g, pr):
        kk = kbuf[jj * blk:jj * blk + span, g * LANES:(g + 1) * LANES]
        qt = qt_ref[pair_lanes(g, pr), jj * blk:(jj + 1) * blk]
        w = jnp.concatenate([jnp.where(low_rows, qt, zero), jnp.where(low_rows, zero, qt)], axis=1)
        return jnp.dot(kk, w, preferred_element_type=F32)

    def softmax(s, jj, g, pr):
        s_prev = s[0:blk] + (first_prev_bias if jj == 0 else prev_bias)
        s_mid = s[blk:2 * blk]
        s_next = s[2 * blk:] + (last_next_bias if jj == n_blocks - 1 else next_bias)
        head = g * A_GROUP + 2 * pr
        sink = jnp.concatenate([jnp.full((1, blk), sink_ref[head] * LOG2_E, F32),
                                jnp.full((1, blk), sink_ref[head + 1] * LOG2_E, F32)], axis=1)
        m = jnp.maximum(jnp.maximum(jnp.max(s_prev, axis=0, keepdims=True),
                                    jnp.max(s_mid, axis=0, keepdims=True)),
                        jnp.maximum(jnp.max(s_next, axis=0, keepdims=True), sink))
        p = jnp.concatenate([jnp.exp2(s_prev - m), jnp.exp2(s_mid - m), jnp.exp2(s_next - m)], axis=0)
        return p.astype(BF16), jnp.exp2(sink - m)

    def weighted_values(p, sink_term, jj, g, pr):
        vt = vbuf[g * V_ROWS:(g + 1) * V_ROWS, jj * blk:jj * blk + span]
        ot = jnp.dot(vt, p, preferred_element_type=F32)
        denom = ot[A_HEAD_DIM:A_HEAD_DIM + 1, :] + sink_term
        ot = ot[0:A_HEAD_DIM, :] * (1.0 / denom)
        both = jnp.concatenate([ot[:, 0:blk], ot[:, blk:]], axis=0)
        o_ref[jj * blk:(jj + 1) * blk, pair_lanes(g, pr)] = both.T.astype(BF16)

    s_stage = {}
    p_stage = {}
    width = 4
    n_steps = len(units) // width
    for n in range(n_steps + 2):
        for u in range(width):
            if n < n_steps:
                s_stage[n * width + u] = scores(*units[n * width + u])
        for u in range(width):
            if 0 <= n - 1 < n_steps:
                i1 = (n - 1) * width + u
                p_stage[i1] = softmax(s_stage.pop(i1), *units[i1])
        for u in range(width):
            if 0 <= n - 2 < n_steps:
                i2 = (n - 2) * width + u
                weighted_values(*p_stage.pop(i2), *units[i2])


def _window_attention(qt, kd, vt, sink, batch, seq_len):
    q_cols, n = qt.shape
    v_cols = vt.shape[0]
    tq = ATTN_Q_TILE
    blk = WINDOW
    nt = seq_len // tq
    per_tile = tq // blk
    nblk = seq_len // blk
    k_dup = kd.shape[1]

    def main_idx(b, i):
        return b * nt + i

    def prev_idx(b, i):
        return b * nblk + jnp.maximum(i * per_tile - 1, 0)

    def next_idx(b, i):
        return b * nblk + jnp.minimum((i + 1) * per_tile, nblk - 1)

    k_specs = [pl.BlockSpec((blk, k_dup), lambda b, i: (prev_idx(b, i), 0)),
               pl.BlockSpec((tq, k_dup), lambda b, i: (main_idx(b, i), 0)),
               pl.BlockSpec((blk, k_dup), lambda b, i: (next_idx(b, i), 0))]
    v_specs = [pl.BlockSpec((v_cols, blk), lambda b, i: (0, prev_idx(b, i))),
               pl.BlockSpec((v_cols, tq), lambda b, i: (0, main_idx(b, i))),
               pl.BlockSpec((v_cols, blk), lambda b, i: (0, next_idx(b, i)))]
    return pl.pallas_call(
        _window_attn_kernel,
        grid=(batch, nt),
        in_specs=[pl.BlockSpec(memory_space=pltpu.SMEM),
                  pl.BlockSpec((q_cols, tq), lambda b, i: (0, main_idx(b, i)))] + k_specs + v_specs,
        out_specs=pl.BlockSpec((tq, q_cols), lambda b, i: (main_idx(b, i), 0)),
        out_shape=jax.ShapeDtypeStruct((n, q_cols), BF16),
        scratch_shapes=[pltpu.VMEM((tq + 2 * blk, k_dup), BF16),
                        pltpu.VMEM((A_KV_HEADS * V_ROWS, tq + 2 * blk), BF16)],
        compiler_params=_compiler_params(("parallel", "parallel")),
        name="window_attn",
    )(sink, qt, kd, kd, kd, vt, vt, vt)


def _residual_mlp_tail(mixer_out, x_ref, wo_ref, ln1g_ref, ln1b_ref, w1_ref, w2_ref, ln2g_ref, ln2b_ref,
                       out_ref):
    tm = x_ref.shape[0]
    parts = [slice(lo, lo + POST_PART_ROWS) for lo in range(0, tm, POST_PART_ROWS)]
    n_ff = D_FF // FF_CHUNK

    def project(rows):
        return jnp.dot(mixer_out(rows), wo_ref[...], preferred_element_type=F32)

    def norm1(y, rows):
        x1 = _layer_norm(DN_ALPHA * x_ref[rows, :] + y, ln1g_ref[...], ln1b_ref[...])
        return x1, x1.astype(BF16)

    def mlp_chunk(x1b, acc, c):
        hc = jnp.dot(x1b, w1_ref[:, c * FF_CHUNK:(c + 1) * FF_CHUNK], preferred_element_type=F32)
        hc = jnp.square(jnp.maximum(hc, 0.0)).astype(BF16)
        part = jnp.dot(hc, w2_ref[c * FF_CHUNK:(c + 1) * FF_CHUNK, :], preferred_element_type=F32)
        return part if acc is None else acc + part

    def norm2(x1, acc, rows):
        out_ref[rows, :] = _layer_norm(DN_ALPHA * x1 + acc, ln2g_ref[...], ln2b_ref[...])

    n_parts = len(parts)
    normed = {0: norm1(project(parts[0]), parts[0])}
    done = None
    for k in range(n_parts):
        x1, x1b = normed.pop(k)
        if k + 1 < n_parts:
            y_next = project(parts[k + 1])
        acc = mlp_chunk(x1b, None, 0)
        if done is not None:
            norm2(*done)
        acc = mlp_chunk(x1b, acc, 1)
        if k + 1 < n_parts:
            normed[k + 1] = norm1(y_next, parts[k + 1])
        for c in range(2, n_ff):
            acc = mlp_chunk(x1b, acc, c)
        done = (x1, acc, parts[k])
    norm2(*done)


def _attn_post_kernel(o_ref, *tail_refs):
    _residual_mlp_tail(lambda rows: o_ref[rows, :], *tail_refs)


def _gla_post_kernel(of_ref, ob_ref, r_ref, ng_ref, *tail_refs):
    def mixer_out(rows):
        o = of_ref[rows, :].astype(F32) + ob_ref[rows, :].astype(F32)
        r = r_ref[rows, :]
        gate = r * (1.0 / (1.0 + jnp.exp(-r)))
        ng = ng_ref[...]
        heads = []
        for h in range(B_HEADS):
            oh = o[:, h * B_VAL_DIM:(h + 1) * B_VAL_DIM]
            oh = oh * lax.rsqrt(jnp.mean(oh * oh, axis=-1, keepdims=True) + HEAD_NORM_EPS)
            heads.append((oh * ng * gate[:, h * B_VAL_DIM:(h + 1) * B_VAL_DIM]).astype(BF16))
        return jnp.concatenate(heads, axis=-1)

    _residual_mlp_tail(mixer_out, *tail_refs)


def _post_mixer(kernel_fn, tm, mixer_inputs, extra_consts, x2, wo, ln1g, ln1b, w1, w2, ln2g, ln2b):
    n = x2.shape[0]
    tile = lambda cols: pl.BlockSpec((tm, cols), lambda i: (i, 0))
    consts = list(extra_consts)
    return pl.pallas_call(
        kernel_fn,
        grid=(n // tm,),
        in_specs=[tile(a.shape[1]) for a in mixer_inputs]
        + [_const_spec(c.shape) for c in consts]
        + [tile(D_MODEL), _const_spec(wo.shape), _const_spec(ln1g.shape), _const_spec(ln1b.shape),
           _const_spec(w1.shape), _const_spec(w2.shape), _const_spec(ln2g.shape), _const_spec(ln2b.shape)],
        out_specs=tile(D_MODEL),
        out_shape=jax.ShapeDtypeStruct((n, D_MODEL), F32),
        compiler_params=_compiler_params(("parallel",)),
        name=kernel_fn.__name__.strip("_"),
    )(*mixer_inputs, *consts, x2, wo, ln1g, ln1b, w1, w2, ln2g, ln2b)


def _gla_in_kernel(x_ref, w_ref, w2f_ref, gbf_ref, w2b_ref, gbb_ref,
                   v_ref, r_ref, qef_ref, kef_ref, kdf_ref, qeb_ref, keb_ref, kdb_ref, tot_ref):
    xb = x_ref[...].astype(BF16)
    tm = x_ref.shape[0]
    c1 = B_QK_COLS
    c2 = 2 * B_QK_COLS
    c3 = c2 + B_V_COLS
    c4 = c3 + B_V_COLS
    n_chunks = tm // B_CHUNK

    def project(lo, hi):
        return jnp.dot(xb, w_ref[:, lo:hi], preferred_element_type=F32)

    def chunk_totals(b, reverse):
        edge = 0 if reverse else B_CHUNK - 1
        return [b[c * B_CHUNK + edge:c * B_CHUNK + edge + 1, :] for c in range(n_chunks)]

    def decayed(q, k, b, totals, qe_ref, ke_ref, kd_ref):
        decay_total = jnp.concatenate(
            [jnp.broadcast_to(jnp.exp2(t), (B_CHUNK, t.shape[1])) for t in totals], axis=0)
        ke = k * jnp.exp2(-b)
        qe_ref[...] = (q * jnp.exp2(b)).astype(BF16)
        ke_ref[...] = ke.astype(BF16)
        kd_ref[...] = (ke * decay_total).astype(BF16)

    half_v = B_V_COLS // 2
    lr = project(c4, c4 + LANES).astype(BF16)
    q = project(0, c1) * (B_KEY_DIM ** -0.5)
    g_f = _log_gate(lr, w2f_ref[...], gbf_ref[...])
    k = project(c1, c2)
    g_b = _log_gate(lr, w2b_ref[...], gbb_ref[...])
    v_ref[:, :half_v] = project(c2, c2 + half_v).astype(BF16)
    b_f = _chunk_cumsum(g_f, False)
    totals_f = chunk_totals(b_f, False)
    v_ref[:, half_v:] = project(c2 + half_v, c3).astype(BF16)
    decayed(q, k, b_f, totals_f, qef_ref, kef_ref, kdf_ref)
    r_ref[:, :half_v] = project(c3, c3 + half_v)
    b_b = _chunk_cumsum(g_b, True)
    totals_b = chunk_totals(b_b, True)
    decayed(q, k, b_b, totals_b, qeb_ref, keb_ref, kdb_ref)
    r_ref[:, half_v:] = project(c3 + half_v, c4)
    tot_ref[0] = jnp.concatenate(totals_f + totals_b, axis=0)


def _gla_in_proj(x2, w, w2f, gbf, w2b, gbb):
    n = x2.shape[0]
    tm = TOKEN_TILE
    n_tot = 2 * (tm // B_CHUNK)
    tile = lambda cols: pl.BlockSpec((tm, cols), lambda i: (i, 0))
    qk = jax.ShapeDtypeStruct((n, B_QK_COLS), BF16)
    return pl.pallas_call(
        _gla_in_kernel,
        grid=(n // tm,),
        in_specs=[tile(D_MODEL), _const_spec(w.shape), _const_spec(w2f.shape), _const_spec(gbf.shape),
                  _const_spec(w2b.shape), _const_spec(gbb.shape)],
        out_specs=[tile(B_V_COLS), tile(B_V_COLS)] + [tile(B_QK_COLS)] * 6
        + [pl.BlockSpec((1, n_tot, B_QK_COLS), lambda i: (i, 0, 0))],
        out_shape=[jax.ShapeDtypeStruct((n, B_V_COLS), BF16), jax.ShapeDtypeStruct((n, B_V_COLS), F32)]
        + [qk] * 6 + [jax.ShapeDtypeStruct((n // tm, n_tot, B_QK_COLS), F32)],
        compiler_params=_compiler_params(("parallel",)),
        name="gla_in_proj",
    )(x2, w, w2f, gbf, w2b, gbb)


def _log_gate(lr, w2, bias):
    z = jnp.dot(lr, w2, preferred_element_type=F32) + bias
    log_term = jnp.log2(1.0 + jnp.exp(-jnp.abs(z)))
    return jnp.minimum(z, 0.0) * (LOG2_E / B_GATE_TAU) - log_term * (1.0 / B_GATE_TAU)


def _chunk_cumsum(g, reverse):
    hi = g.astype(BF16)
    rest = g - hi.astype(F32)
    mid = rest.astype(BF16)
    lo = (rest - mid.astype(F32)).astype(BF16)
    t_row = lax.broadcasted_iota(jnp.int32, (B_CHUNK, 3 * B_CHUNK), 0)
    t_col = lax.broadcasted_iota(jnp.int32, (B_CHUNK, 3 * B_CHUNK), 1) % B_CHUNK
    keep = (t_row <= t_col) if reverse else (t_row >= t_col)
    tri = jnp.where(keep, 1.0, 0.0).astype(BF16)
    out = []
    for c in range(g.shape[0] // B_CHUNK):
        rows = slice(c * B_CHUNK, (c + 1) * B_CHUNK)
        terms = jnp.concatenate([hi[rows], mid[rows], lo[rows]], axis=0)
        out.append(jnp.dot(tri, terms, preferred_element_type=F32))
    return jnp.concatenate(out, axis=0)


def _gla_scan_kernel(qef_ref, kef_ref, kdf_ref, vf_ref, totf_ref, qeb_ref, keb_ref, kdb_ref, vb_ref, totb_ref,
                     of_ref, ob_ref, d_ref, sf_ref, sb_ref):
    @pl.when(pl.program_id(1) == 0)
    def _():
        sf_ref[...] = jnp.zeros_like(sf_ref)
        sb_ref[...] = jnp.zeros_like(sb_ref)

    n_chunks = GLA_TILE // B_CHUNK
    pair_k = 2 * B_KEY_DIM
    sub_chunks = TOKEN_TILE // B_CHUNK
    totals = jnp.concatenate([totf_ref[t, 0:sub_chunks, :] for t in range(totf_ref.shape[0])]
                             + [totb_ref[t, sub_chunks:, :] for t in range(totb_ref.shape[0])], axis=0)
    d_ref[...] = jnp.exp2(totals).T

    directions = ((False, qef_ref, kef_ref, kdf_ref, vf_ref, 0, of_ref, sf_ref),
                  (True, qeb_ref, keb_ref, kdb_ref, vb_ref, n_chunks, ob_ref, sb_ref))
    t_row = lax.broadcasted_iota(jnp.int32, (B_CHUNK, 2 * B_CHUNK), 0)
    t_col = _lane_index((B_CHUNK, 2 * B_CHUNK)) % B_CHUNK
    first_head_lanes = _lane_index((B_CHUNK, pair_k)) < B_KEY_DIM
    zero_k = jnp.zeros((B_CHUNK, pair_k), BF16)
    zero_v = jnp.zeros((B_CHUNK, B_VAL_DIM), BF16)
    units = [(c, d, hp) for c in range(n_chunks) for d in range(2) for hp in range(B_HEADS // 2)]

    def chunk_rows(c, reverse):
        cc = n_chunks - 1 - c if reverse else c
        return cc, slice(cc * B_CHUNK, (cc + 1) * B_CHUNK)

    def prepare(c, d, hp):
        reverse, qe_ref, ke_ref, kd_ref, v_ref = directions[d][:5]
        _, rows = chunk_rows(c, reverse)
        cols = slice(hp * pair_k, (hp + 1) * pair_k)
        qe = qe_ref[rows, cols]
        ke = ke_ref[rows, cols]
        kd = kd_ref[rows, cols]
        ke_bd = jnp.concatenate([jnp.where(first_head_lanes, ke, zero_k),
                                 jnp.where(first_head_lanes, zero_k, ke)], axis=0)
        att = lax.dot_general(qe, ke_bd, (((1,), (1,)), ((), ())), preferred_element_type=F32)
        causal = (t_row <= t_col) if reverse else (t_row >= t_col)
        att = jnp.where(causal, att, 0.0).astype(BF16)
        updates = []
        for hh in range(2):
            h = 2 * hp + hh
            vh = v_ref[rows, h * B_VAL_DIM:(h + 1) * B_VAL_DIM]
            updates.append(lax.dot_general(kd[:, hh * B_KEY_DIM:(hh + 1) * B_KEY_DIM], vh,
                                           (((0,), (0,)), ((), ())), preferred_element_type=F32))
        return qe, att, updates

    def apply(prepared, c, d, hp):
        reverse, _, _, _, v_ref, d_col0, o_ref, s_ref = directions[d]
        qe, att, updates = prepared
        cc, rows = chunk_rows(c, reverse)
        for hh in range(2):
            h = 2 * hp + hh
            vs = slice(h * B_VAL_DIM, (h + 1) * B_VAL_DIM)
            vh = v_ref[rows, vs]
            state = s_ref[h]
            lhs = jnp.concatenate([qe[:, hh * B_KEY_DIM:(hh + 1) * B_KEY_DIM], att], axis=1)
            rhs = jnp.concatenate([state.astype(BF16)] + ([vh, zero_v] if hh == 0 else [zero_v, vh]), axis=0)
            o_ref[rows, vs] = jnp.dot(lhs, rhs, preferred_element_type=F32).astype(o_ref.dtype)
            decay = d_ref[h * B_KEY_DIM:(h + 1) * B_KEY_DIM, d_col0 + cc:d_col0 + cc + 1]
            s_ref[h] = state * decay + updates[hh]

    prepared = {}
    width = 2
    n_steps = len(units) // width
    for n in range(n_steps + 1):
        for u in range(width):
            if n < n_steps:
                prepared[n * width + u] = prepare(*units[n * width + u])
        for u in range(width):
            if n >= 1:
                i1 = (n - 1) * width + u
                apply(prepared.pop(i1), *units[i1])


def _gla_scan(v, decayed_f, decayed_b, tot, batch, seq_len):
    n = v.shape[0]
    ts = GLA_TILE
    nt = seq_len // ts
    tiles_in = ts // TOKEN_TILE
    n_tot = tot.shape[1]

    def fwd_tile(b, i):
        return b * nt + i

    def bwd_tile(b, i):
        return b * nt + nt - 1 - i

    def token_specs(tile_of):
        rows = lambda b, i: (tile_of(b, i), 0)
        return [pl.BlockSpec((ts, B_QK_COLS), rows)] * 3 + [
            pl.BlockSpec((ts, B_V_COLS), rows),
            pl.BlockSpec((tiles_in, n_tot, B_QK_COLS), lambda b, i: (tile_of(b, i), 0, 0))]

    return pl.pallas_call(
        _gla_scan_kernel,
        grid=(batch, nt),
        in_specs=token_specs(fwd_tile) + token_specs(bwd_tile),
        out_specs=[pl.BlockSpec((ts, B_V_COLS), lambda b, i: (fwd_tile(b, i), 0)),
                   pl.BlockSpec((ts, B_V_COLS), lambda b, i: (bwd_tile(b, i), 0))],
        out_shape=[jax.ShapeDtypeStruct((n, B_V_COLS), BF16), jax.ShapeDtypeStruct((n, B_V_COLS), BF16)],
        scratch_shapes=[
            pltpu.VMEM((B_QK_COLS, tiles_in * n_tot), F32),
            pltpu.VMEM((B_HEADS, B_KEY_DIM, B_VAL_DIM), F32),
            pltpu.VMEM((B_HEADS, B_KEY_DIM, B_VAL_DIM), F32),
        ],
        compiler_params=_compiler_params(("parallel", "arbitrary")),
        name="gla_scan",
    )(*decayed_f, v, tot, *decayed_b, v, tot)


def _row(v):
    return v.reshape(1, -1).astype(F32)


def _pad_gate_weight(w2, first_row):
    out = jnp.zeros((LANES, w2.shape[1]), BF16)
    return lax.dynamic_update_slice(out, w2.astype(BF16), (first_row, 0))


def kernel(x, positions, attn_w_in, attn_sink, attn_w_out, gla_w_in, gla_gate_w2_fwd, gla_gate_b_fwd,
           gla_gate_w2_bwd, gla_gate_b_bwd, gla_norm_g, gla_w_out, mix_ln_g, mix_ln_b, mlp_w1, mlp_w2,
           mlp_ln_g, mlp_ln_b):
    batch, seq_len, d_model = x.shape
    n = batch * seq_len
    assert d_model == D_MODEL
    assert seq_len % ATTN_Q_TILE == 0 and seq_len % GLA_TILE == 0 and n % ATTN_POST_TILE == 0
    x2 = x.reshape(n, d_model)

    inv_freq = ROPE_THETA ** (-jnp.arange(0, A_HEAD_DIM, 2, dtype=F32) / A_HEAD_DIM)
    ang_t = inv_freq[:, None] * positions.astype(F32).reshape(1, n)
    q_cols = A_HEADS * A_HEAD_DIM
    k_cols = A_KV_HEADS * A_HEAD_DIM

    for i in range(DEPTH):
        j = i // 2
        w1 = mlp_w1[i].astype(BF16)
        w2 = mlp_w2[i].astype(BF16)
        tail = (_row(mix_ln_g[i]), _row(mix_ln_b[i]), w1, w2, _row(mlp_ln_g[i]), _row(mlp_ln_b[i]))
        if i % 2 == 0:
            w_in = attn_w_in[j].astype(BF16)
            wqv_t = jnp.concatenate([w_in[:, :q_cols], w_in[:, q_cols + k_cols:]], axis=1).T
            qt, kd, vt = _attn_in_proj(x2, ang_t, wqv_t, w_in[:, q_cols:q_cols + k_cols])
            o = _window_attention(qt, kd, vt, attn_sink[j].astype(F32), batch, seq_len)
            x2 = _post_mixer(_attn_post_kernel, ATTN_POST_TILE, [o], [], x2, attn_w_out[j].astype(BF16), *tail)
        else:
            w_in = gla_w_in[j]
            lr_cols = 2 * B_GATE_RANK
            w_in = jnp.pad(w_in, ((0, 0), (0, LANES - lr_cols))).astype(BF16)
            v, r, *decayed, tot = _gla_in_proj(
                x2, w_in, _pad_gate_weight(gla_gate_w2_fwd[j], 0), _row(gla_gate_b_fwd[j]),
                _pad_gate_weight(gla_gate_w2_bwd[j], B_GATE_RANK), _row(gla_gate_b_bwd[j]))
            o_f, o_b = _gla_scan(v, decayed[:3], decayed[3:], tot, batch, seq_len)
            x2 = _post_mixer(_gla_post_kernel, TOKEN_TILE, [o_f, o_b, r], [_row(gla_norm_g[j])], x2,
                             gla_w_out[j].astype(BF16), *tail)
    return x2.reshape(batch, seq_len, d_model)
```

```python
import functools

import jax
import jax.numpy as jnp
from jax import lax
from jax.experimental import pallas as pl
from jax.experimental.pallas import tpu as pltpu

D_MODEL = 1024
DEPTH = 2

A_HEADS = 16
A_KV_HEADS = 4
A_HEAD_DIM = D_MODEL // A_HEADS
A_GROUP = A_HEADS // A_KV_HEADS
WINDOW = 128
ROPE_THETA = 10000.0

B_HEADS = 4
B_KEY_DIM = (D_MODEL // 2) // B_HEADS
B_VAL_DIM = D_MODEL // B_HEADS
B_GATE_RANK = 16
B_GATE_TAU = 16.0
B_CHUNK = 64
B_QK_COLS = B_HEADS * B_KEY_DIM
B_V_COLS = B_HEADS * B_VAL_DIM

D_FF = 4 * D_MODEL
DN_ALPHA = float((2 * DEPTH) ** 0.25)
LN_EPS = 1e-5
HEAD_NORM_EPS = 1e-6
LOG2_E = 1.4426950408889634

LANES = 128
BF16_ROWS = 16
V_ROWS = A_HEAD_DIM + BF16_ROWS
VMEM_LIMIT_BYTES = 56 * 1024 * 1024

TOKEN_TILE = 512
ATTN_IN_TILE = 1024
ATTN_Q_TILE = 2048
GLA_TILE = 2 * TOKEN_TILE
FF_CHUNK = 1024
POST_PART_ROWS = 256
ATTN_POST_TILE = TOKEN_TILE

BF16 = jnp.bfloat16
F32 = jnp.float32


def _const_spec(shape):
    zeros = (0,) * len(shape)
    return pl.BlockSpec(shape, lambda *_: zeros, pipeline_mode=pl.Buffered(1))


def _compiler_params(semantics, flags=None):
    return pltpu.CompilerParams(dimension_semantics=semantics, vmem_limit_bytes=VMEM_LIMIT_BYTES, flags=flags)


def _layer_norm(z, g, b):
    mu = jnp.mean(z, axis=-1, keepdims=True)
    zc = z - mu
    var = jnp.mean(zc * zc, axis=-1, keepdims=True)
    return zc * lax.rsqrt(var + LN_EPS) * g + b


def _lane_index(shape):
    return lax.broadcasted_iota(jnp.int32, shape, len(shape) - 1)


def _attn_in_kernel(x_ref, ang_ref, wqv_ref, wk_ref, qt_ref, k_ref, vt_ref):
    xb = x_ref[...].astype(BF16)
    half = A_HEAD_DIM // 2
    q_cols = A_HEADS * A_HEAD_DIM
    kv_cols = A_KV_HEADS * A_HEAD_DIM
    scale = A_HEAD_DIM ** -0.5 * LOG2_E
    block = 4 * A_HEAD_DIM

    def project_t(lo):
        return lax.dot_general(wqv_ref[lo:lo + block, :], xb, (((1,), (1,)), ((), ())),
                               preferred_element_type=F32)

    def rotate_q(hq, lo):
        for h in range(block // A_HEAD_DIM):
            r0 = h * A_HEAD_DIM
            t1 = hq[r0:r0 + half, :]
            t2 = hq[r0 + half:r0 + A_HEAD_DIM, :]
            qt_ref[lo + r0:lo + r0 + half, :] = ((t1 * cos_t - t2 * sin_t) * scale).astype(BF16)
            qt_ref[lo + r0 + half:lo + r0 + A_HEAD_DIM, :] = ((t2 * cos_t + t1 * sin_t) * scale).astype(BF16)

    def rotate_k(hk):
        cos = jnp.concatenate([cos_t] * (LANES // half), axis=0).T
        sin = jnp.concatenate([sin_t] * (LANES // half), axis=0).T
        lane = _lane_index(cos.shape)
        first_half = (lane % A_HEAD_DIM) < half
        sin_signed = jnp.where(first_half, -sin, sin)
        low_head = lane < A_HEAD_DIM
        for j in range(kv_cols // LANES):
            t = hk[:, j * LANES:(j + 1) * LANES]
            partner = jnp.where(first_half, pltpu.roll(t, LANES - half, 1), pltpu.roll(t, half, 1))
            t = t * cos + partner * sin_signed
            swapped = pltpu.roll(t, A_HEAD_DIM, 1)
            k_ref[:, (2 * j) * LANES:(2 * j + 1) * LANES] = jnp.where(low_head, t, swapped).astype(BF16)
            k_ref[:, (2 * j + 1) * LANES:(2 * j + 2) * LANES] = jnp.where(low_head, swapped, t).astype(BF16)

    hk = jnp.dot(xb, wk_ref[...], preferred_element_type=F32)
    cos_t = jnp.cos(ang_ref[...])
    sin_t = jnp.sin(ang_ref[...])
    pieces = {0: project_t(0)}
    rotate_k(hk)
    for p in range(1, q_cols // block):
        pieces[p] = project_t(p * block)
        rotate_q(pieces.pop(p - 1), (p - 1) * block)
    hv = project_t(q_cols)
    rotate_q(pieces.pop(q_cols // block - 1), q_cols - block)
    vt_ref[...] = hv.astype(BF16)


def _attn_in_proj(x2, ang_t, wqv_t, wk):
    n = x2.shape[0]
    tm = ATTN_IN_TILE
    q_cols = A_HEADS * A_HEAD_DIM
    kv_cols = A_KV_HEADS * A_HEAD_DIM
    k_dup = A_KV_HEADS * LANES
    return pl.pallas_call(
        _attn_in_kernel,
        grid=(n // tm,),
        in_specs=[
            pl.BlockSpec((tm, D_MODEL), lambda i: (i, 0)),
            pl.BlockSpec((ang_t.shape[0], tm), lambda i: (0, i)),
            _const_spec(wqv_t.shape),
            _const_spec(wk.shape),
        ],
        out_specs=[
            pl.BlockSpec((q_cols, tm), lambda i: (0, i)),
            pl.BlockSpec((tm, k_dup), lambda i: (i, 0)),
            pl.BlockSpec((kv_cols, tm), lambda i: (0, i)),
        ],
        out_shape=[
            jax.ShapeDtypeStruct((q_cols, n), BF16),
            jax.ShapeDtypeStruct((n, k_dup), BF16),
            jax.ShapeDtypeStruct((kv_cols, n), BF16),
        ],
        compiler_params=_compiler_params(("parallel",)),
        name="attn_in_proj",
    )(x2, ang_t, wqv_t, wk)


def _window_attn_kernel(sink_ref, qt_ref, kp_ref, km_ref, kn_ref, vp_ref, vm_ref, vn_ref, o_ref,
                        kbuf, vbuf):
    tq = ATTN_Q_TILE
    blk = WINDOW
    span = 3 * blk
    i = pl.program_id(1)
    last_tile = pl.num_programs(1) - 1
    kbuf[0:blk, :] = kp_ref[...]
    kbuf[blk:blk + tq, :] = km_ref[...]
    kbuf[blk + tq:, :] = kn_ref[...]
    for g in range(A_KV_HEADS):
        src = slice(g * A_HEAD_DIM, (g + 1) * A_HEAD_DIM)
        dst = slice(g * V_ROWS, g * V_ROWS + A_HEAD_DIM)
        vbuf[dst, 0:blk] = vp_ref[src, :]
        vbuf[dst, blk:blk + tq] = vm_ref[src, :]
        vbuf[dst, blk + tq:] = vn_ref[src, :]
        vbuf[g * V_ROWS + A_HEAD_DIM:(g + 1) * V_ROWS, :] = jnp.ones((BF16_ROWS, tq + 2 * blk), BF16)

    pair_shape = (blk, 2 * LANES)
    key = lax.broadcasted_iota(jnp.int32, pair_shape, 0)
    query = _lane_index(pair_shape) % blk
    neg_inf = jnp.float32(-jnp.inf)
    prev_bias = jnp.where(key >= query, 0.0, neg_inf)
    next_bias = jnp.where(key <= query, 0.0, neg_inf)
    first_prev_bias = jnp.where(i == 0, neg_inf, prev_bias)
    last_next_bias = jnp.where(i == last_tile, neg_inf, next_bias)
    low_rows = lax.broadcasted_iota(jnp.int32, (LANES, blk), 0) < A_HEAD_DIM
    zero = jnp.zeros((LANES, blk), BF16)

    n_blocks = tq // blk
    units = [(jj, g, pr) for jj in range(n_blocks) for g in range(A_KV_HEADS) for pr in range(A_GROUP // 2)]

    def pair_lanes(g, pr):
        return slice((g * 2 + pr) * LANES, (g * 2 + pr + 1) * LANES)

    def scores(jj, g, pr):
        kk = kbuf[jj * blk:jj * blk + span, g * LANES:(g + 1) * LANES]
        qt = qt_ref[pair_lanes(g, pr), jj * blk:(jj + 1) * blk]
        w = jnp.concatenate([jnp.where(low_rows, qt, zero), jnp.where(low_rows, zero, qt)], axis=1)
        return jnp.dot(kk, w, preferred_element_type=F32)

    def softmax(s, jj, g, pr):
        s_prev = s[0:blk] + (first_prev_bias if jj == 0 else prev_bias)
        s_mid = s[blk:2 * blk]
        s_next = s[2 * blk:] + (last_next_bias if jj == n_blocks - 1 else next_bias)
        head = g * A_GROUP + 2 * pr
        sink = jnp.concatenate([jnp.full((1, blk), sink_ref[head] * LOG2_E, F32),
                                jnp.full((1, blk), sink_ref[head + 1] * LOG2_E, F32)], axis=1)
        m = jnp.maximum(jnp.maximum(jnp.max(s_prev, axis=0, keepdims=True),
                                    jnp.max(s_mid, axis=0, keepdims=True)),
                        jnp.maximum(jnp.max(s_next, axis=0, keepdims=True), sink))
        p = jnp.concatenate([jnp.exp2(s_prev - m), jnp.exp2(s_mid - m), jnp.exp2(s_next - m)], axis=0)
        return p.astype(BF16), jnp.exp2(sink - m)

    def weighted_values(p, sink_term, jj, g, pr):
        vt = vbuf[g * V_ROWS:(g + 1) * V_ROWS, jj * blk:jj * blk + span]
        ot = jnp.dot(vt, p, preferred_element_type=F32)
        denom = ot[A_HEAD_DIM:A_HEAD_DIM + 1, :] + sink_term
        ot = ot[0:A_HEAD_DIM, :] * (1.0 / denom)
        both = jnp.concatenate([ot[:, 0:blk], ot[:, blk:]], axis=0)
        o_ref[jj * blk:(jj + 1) * blk, pair_lanes(g, pr)] = both.T.astype(BF16)

    s_stage = {}
    p_stage = {}
    width = 4
    n_steps = len(units) // width
    for n in range(n_steps + 2):
        for u in range(width):
            if n < n_steps:
                s_stage[n * width + u] = scores(*units[n * width + u])
        for u in range(width):
            if 0 <= n - 1 < n_steps:
                i1 = (n - 1) * width + u
                p_stage[i1] = softmax(s_stage.pop(i1), *units[i1])
        for u in range(width):
            if 0 <= n - 2 < n_steps:
                i2 = (n - 2) * width + u
                weighted_values(*p_stage.pop(i2), *units[i2])


def _window_attention(qt, kd, vt, sink, batch, seq_len):
    q_cols, n = qt.shape
    v_cols = vt.shape[0]
    tq = ATTN_Q_TILE
    blk = WINDOW
    nt = seq_len // tq
    per_tile = tq // blk
    nblk = seq_len // blk
    k_dup = kd.shape[1]

    def main_idx(b, i):
        return b * nt + i

    def prev_idx(b, i):
        return b * nblk + jnp.maximum(i * per_tile - 1, 0)

    def next_idx(b, i):
        return b * nblk + jnp.minimum((i + 1) * per_tile, nblk - 1)

    k_specs = [pl.BlockSpec((blk, k_dup), lambda b, i: (prev_idx(b, i), 0)),
               pl.BlockSpec((tq, k_dup), lambda b, i: (main_idx(b, i), 0)),
               pl.BlockSpec((blk, k_dup), lambda b, i: (next_idx(b, i), 0))]
    v_specs = [pl.BlockSpec((v_cols, blk), lambda b, i: (0, prev_idx(b, i))),
               pl.BlockSpec((v_cols, tq), lambda b, i: (0, main_idx(b, i))),
               pl.BlockSpec((v_cols, blk), lambda b, i: (0, next_idx(b, i)))]
    return pl.pallas_call(
        _window_attn_kernel,
        grid=(batch, nt),
        in_specs=[pl.BlockSpec(memory_space=pltpu.SMEM),
                  pl.BlockSpec((q_cols, tq), lambda b, i: (0, main_idx(b, i)))] + k_specs + v_specs,
        out_specs=pl.BlockSpec((tq, q_cols), lambda b, i: (main_idx(b, i), 0)),
        out_shape=jax.ShapeDtypeStruct((n, q_cols), BF16),
        scratch_shapes=[pltpu.VMEM((tq + 2 * blk, k_dup), BF16),
                        pltpu.VMEM((A_KV_HEADS * V_ROWS, tq + 2 * blk), BF16)],
        compiler_params=_compiler_params(("parallel", "parallel")),
        name="window_attn",
    )(sink, qt, kd, kd, kd, vt, vt, vt)


def _residual_mlp_tail(mixer_out, x_ref, wo_ref, ln1g_ref, ln1b_ref, w1_ref, w2_ref, ln2g_ref, ln2b_ref,
                       out_ref):
    tm = x_ref.shape[0]
    parts = [slice(lo, lo + POST_PART_ROWS) for lo in range(0, tm, POST_PART_ROWS)]
    n_ff = D_FF // FF_CHUNK

    def project(rows):
        return jnp.dot(mixer_out(rows), wo_ref[...], preferred_element_type=F32)

    def norm1(y, rows):
        x1 = _layer_norm(DN_ALPHA * x_ref[rows, :] + y, ln1g_ref[...], ln1b_ref[...])
        return x1, x1.astype(BF16)

    def mlp_chunk(x1b, acc, c):
        hc = jnp.dot(x1b, w1_ref[:, c * FF_CHUNK:(c + 1) * FF_CHUNK], preferred_element_type=F32)
        hc = jnp.square(jnp.maximum(hc, 0.0)).astype(BF16)
        part = jnp.dot(hc, w2_ref[c * FF_CHUNK:(c + 1) * FF_CHUNK, :], preferred_element_type=F32)
        return part if acc is None else acc + part

    def norm2(x1, acc, rows):
        out_ref[rows, :] = _layer_norm(DN_ALPHA * x1 + acc, ln2g_ref[...], ln2b_ref[...])

    n_parts = len(parts)
    normed = {0: norm1(project(parts[0]), parts[0])}
    done = None
    for k in range(n_parts):
        x1, x1b = normed.pop(k)
        if k + 1 < n_parts:
            y_next = project(parts[k + 1])
        acc = mlp_chunk(x1b, None, 0)
        if done is not None:
            norm2(*done)
        acc = mlp_chunk(x1b, acc, 1)
        if k + 1 < n_parts:
            normed[k + 1] = norm1(y_next, parts[k + 1])
        for c in range(2, n_ff):
            acc = mlp_chunk(x1b, acc, c)
        done = (x1, acc, parts[k])
    norm2(*done)


def _attn_post_kernel(o_ref, *tail_refs):
    _residual_mlp_tail(lambda rows: o_ref[rows, :], *tail_refs)


def _gla_post_kernel(of_ref, ob_ref, r_ref, ng_ref, *tail_refs):
    def mixer_out(rows):
        o = of_ref[rows, :].astype(F32) + ob_ref[rows, :].astype(F32)
        r = r_ref[rows, :]
        gate = r * (1.0 / (1.0 + jnp.exp(-r)))
        ng = ng_ref[...]
        heads = []
        for h in range(B_HEADS):
            oh = o[:, h * B_VAL_DIM:(h + 1) * B_VAL_DIM]
            oh = oh * lax.rsqrt(jnp.mean(oh * oh, axis=-1, keepdims=True) + HEAD_NORM_EPS)
            heads.append((oh * ng * gate[:, h * B_VAL_DIM:(h + 1) * B_VAL_DIM]).astype(BF16))
        return jnp.concatenate(heads, axis=-1)

    _residual_mlp_tail(mixer_out, *tail_refs)


def _post_mixer(kernel_fn, tm, mixer_inputs, extra_consts, x2, wo, ln1g, ln1b, w1, w2, ln2g, ln2b):
    n = x2.shape[0]
    tile = lambda cols: pl.BlockSpec((tm, cols), lambda i: (i, 0))
    consts = list(extra_consts)
    return pl.pallas_call(
        kernel_fn,
        grid=(n // tm,),
        in_specs=[tile(a.shape[1]) for a in mixer_inputs]
        + [_const_spec(c.shape) for c in consts]
        + [tile(D_MODEL), _const_spec(wo.shape), _const_spec(ln1g.shape), _const_spec(ln1b.shape),
           _const_spec(w1.shape), _const_spec(w2.shape), _const_spec(ln2g.shape), _const_spec(ln2b.shape)],
        out_specs=tile(D_MODEL),
        out_shape=jax.ShapeDtypeStruct((n, D_MODEL), F32),
        compiler_params=_compiler_params(("parallel",)),
        name=kernel_fn.__name__.strip("_"),
    )(*mixer_inputs, *consts, x2, wo, ln1g, ln1b, w1, w2, ln2g, ln2b)


def _gla_in_kernel(x_ref, w_ref, w2f_ref, gbf_ref, w2b_ref, gbb_ref,
                   v_ref, r_ref, qef_ref, kef_ref, kdf_ref, qeb_ref, keb_ref, kdb_ref, tot_ref):
    xb = x_ref[...].astype(BF16)
    tm = x_ref.shape[0]
    c1 = B_QK_COLS
    c2 = 2 * B_QK_COLS
    c3 = c2 + B_V_COLS
    c4 = c3 + B_V_COLS
    n_chunks = tm // B_CHUNK

    def project(lo, hi):
        return jnp.dot(xb, w_ref[:, lo:hi], preferred_element_type=F32)

    def chunk_totals(b, reverse):
        edge = 0 if reverse else B_CHUNK - 1
        return [b[c * B_CHUNK + edge:c * B_CHUNK + edge + 1, :] for c in range(n_chunks)]

    def decayed(q, k, b, totals, qe_ref, ke_ref, kd_ref):
        decay_total = jnp.concatenate(
            [jnp.broadcast_to(jnp.exp2(t), (B_CHUNK, t.shape[1])) for t in totals], axis=0)
        ke = k * jnp.exp2(-b)
        qe_ref[...] = (q * jnp.exp2(b)).astype(BF16)
        ke_ref[...] = ke.astype(BF16)
        kd_ref[...] = (ke * decay_total).astype(BF16)

    half_v = B_V_COLS // 2
    lr = project(c4, c4 + LANES).astype(BF16)
    q = project(0, c1) * (B_KEY_DIM ** -0.5)
    g_f = _log_gate(lr, w2f_ref[...], gbf_ref[...])
    k = project(c1, c2)
    g_b = _log_gate(lr, w2b_ref[...], gbb_ref[...])
    v_ref[:, :half_v] = project(c2, c2 + half_v).astype(BF16)
    b_f = _chunk_cumsum(g_f, False)
    totals_f = chunk_totals(b_f, False)
    v_ref[:, half_v:] = project(c2 + half_v, c3).astype(BF16)
    decayed(q, k, b_f, totals_f, qef_ref, kef_ref, kdf_ref)
    r_ref[:, :half_v] = project(c3, c3 + half_v)
    b_b = _chunk_cumsum(g_b, True)
    totals_b = chunk_totals(b_b, True)
    decayed(q, k, b_b, totals_b, qeb_ref, keb_ref, kdb_ref)
    r_ref[:, half_v:] = project(c3 + half_v, c4)
    tot_ref[0] = jnp.concatenate(totals_f + totals_b, axis=0)


def _gla_in_proj(x2, w, w2f, gbf, w2b, gbb):
    n = x2.shape[0]
    tm = TOKEN_TILE
    n_tot = 2 * (tm // B_CHUNK)
    tile = lambda cols: pl.BlockSpec((tm, cols), lambda i: (i, 0))
    qk = jax.ShapeDtypeStruct((n, B_QK_COLS), BF16)
    return pl.pallas_call(
        _gla_in_kernel,
        grid=(n // tm,),
        in_specs=[tile(D_MODEL), _const_spec(w.shape), _const_spec(w2f.shape), _const_spec(gbf.shape),
                  _const_spec(w2b.shape), _const_spec(gbb.shape)],
        out_specs=[tile(B_V_COLS), tile(B_V_COLS)] + [tile(B_QK_COLS)] * 6
        + [pl.BlockSpec((1, n_tot, B_QK_COLS), lambda i: (i, 0, 0))],
        out_shape=[jax.ShapeDtypeStruct((n, B_V_COLS), BF16), jax.ShapeDtypeStruct((n, B_V_COLS), F32)]
        + [qk] * 6 + [jax.ShapeDtypeStruct((n // tm, n_tot, B_QK_COLS), F32)],
        compiler_params=_compiler_params(("parallel",)),
        name="gla_in_proj",
    )(x2, w, w2f, gbf, w2b, gbb)


def _log_gate(lr, w2, bias):
    z = jnp.dot(lr, w2, preferred_element_type=F32) + bias
    log_term = jnp.log2(1.0 + jnp.exp(-jnp.abs(z)))
    return jnp.minimum(z, 0.0) * (LOG2_E / B_GATE_TAU) - log_term * (1.0 / B_GATE_TAU)


def _chunk_cumsum(g, reverse):
    hi = g.astype(BF16)
    rest = g - hi.astype(F32)
    mid = rest.astype(BF16)
    lo = (rest - mid.astype(F32)).astype(BF16)
    t_row = lax.broadcasted_iota(jnp.int32, (B_CHUNK, 3 * B_CHUNK), 0)
    t_col = lax.broadcasted_iota(jnp.int32, (B_CHUNK, 3 * B_CHUNK), 1) % B_CHUNK
    keep = (t_row <= t_col) if reverse else (t_row >= t_col)
    tri = jnp.where(keep, 1.0, 0.0).astype(BF16)
    out = []
    for c in range(g.shape[0] // B_CHUNK):
        rows = slice(c * B_CHUNK, (c + 1) * B_CHUNK)
        terms = jnp.concatenate([hi[rows], mid[rows], lo[rows]], axis=0)
        out.append(jnp.dot(tri, terms, preferred_element_type=F32))
    return jnp.concatenate(out, axis=0)


def _gla_scan_kernel(qef_ref, kef_ref, kdf_ref, vf_ref, totf_ref, qeb_ref, keb_ref, kdb_ref, vb_ref, totb_ref,
                     of_ref, ob_ref, d_ref, sf_ref, sb_ref):
    @pl.when(pl.program_id(1) == 0)
    def _():
        sf_ref[...] = jnp.zeros_like(sf_ref)
        sb_ref[...] = jnp.zeros_like(sb_ref)

    n_chunks = GLA_TILE // B_CHUNK
    pair_k = 2 * B_KEY_DIM
    sub_chunks = TOKEN_TILE // B_CHUNK
    totals = jnp.concatenate([totf_ref[t, 0:sub_chunks, :] for t in range(totf_ref.shape[0])]
                             + [totb_ref[t, sub_chunks:, :] for t in range(totb_ref.shape[0])], axis=0)
    d_ref[...] = jnp.exp2(totals).T

    directions = ((False, qef_ref, kef_ref, kdf_ref, vf_ref, 0, of_ref, sf_ref),
                  (True, qeb_ref, keb_ref, kdb_ref, vb_ref, n_chunks, ob_ref, sb_ref))
    t_row = lax.broadcasted_iota(jnp.int32, (B_CHUNK, 2 * B_CHUNK), 0)
    t_col = _lane_index((B_CHUNK, 2 * B_CHUNK)) % B_CHUNK
    first_head_lanes = _lane_index((B_CHUNK, pair_k)) < B_KEY_DIM
    zero_k = jnp.zeros((B_CHUNK, pair_k), BF16)
    zero_v = jnp.zeros((B_CHUNK, B_VAL_DIM), BF16)
    units = [(c, d, hp) for c in range(n_chunks) for d in range(2) for hp in range(B_HEADS // 2)]

    def chunk_rows(c, reverse):
        cc = n_chunks - 1 - c if reverse else c
        return cc, slice(cc * B_CHUNK, (cc + 1) * B_CHUNK)

    def prepare(c, d, hp):
        reverse, qe_ref, ke_ref, kd_ref, v_ref = directions[d][:5]
        _, rows = chunk_rows(c, reverse)
        cols = slice(hp * pair_k, (hp + 1) * pair_k)
        qe = qe_ref[rows, cols]
        ke = ke_ref[rows, cols]
        kd = kd_ref[rows, cols]
        ke_bd = jnp.concatenate([jnp.where(first_head_lanes, ke, zero_k),
                                 jnp.where(first_head_lanes, zero_k, ke)], axis=0)
        att = lax.dot_general(qe, ke_bd, (((1,), (1,)), ((), ())), preferred_element_type=F32)
        causal = (t_row <= t_col) if reverse else (t_row >= t_col)
        att = jnp.where(causal, att, 0.0).astype(BF16)
        updates = []
        for hh in range(2):
            h = 2 * hp + hh
            vh = v_ref[rows, h * B_VAL_DIM:(h + 1) * B_VAL_DIM]
            updates.append(lax.dot_general(kd[:, hh * B_KEY_DIM:(hh + 1) * B_KEY_DIM], vh,
                                           (((0,), (0,)), ((), ())), preferred_element_type=F32))
        return qe, att, updates

    def apply(prepared, c, d, hp):
        reverse, _, _, _, v_ref, d_col0, o_ref, s_ref = directions[d]
        qe, att, updates = prepared
        cc, rows = chunk_rows(c, reverse)
        for hh in range(2):
            h = 2 * hp + hh
            vs = slice(h * B_VAL_DIM, (h + 1) * B_VAL_DIM)
            vh = v_ref[rows, vs]
            state = s_ref[h]
            lhs = jnp.concatenate([qe[:, hh * B_KEY_DIM:(hh + 1) * B_KEY_DIM], att], axis=1)
            rhs = jnp.concatenate([state.astype(BF16)] + ([vh, zero_v] if hh == 0 else [zero_v, vh]), axis=0)
            o_ref[rows, vs] = jnp.dot(lhs, rhs, preferred_element_type=F32).astype(o_ref.dtype)
            decay = d_ref[h * B_KEY_DIM:(h + 1) * B_KEY_DIM, d_col0 + cc:d_col0 + cc + 1]
            s_ref[h] = state * decay + updates[hh]

    prepared = {}
    width = 2
    n_steps = len(units) // width
    for n in range(n_steps + 1):
        for u in range(width):
            if n < n_steps:
                prepared[n * width + u] = prepare(*units[n * width + u])
        for u in range(width):
            if n >= 1:
                i1 = (n - 1) * width + u
                apply(prepared.pop(i1), *units[i1])


def _gla_scan(v, decayed_f, decayed_b, tot, batch, seq_len):
    n = v.shape[0]
    ts = GLA_TILE
    nt = seq_len // ts
    tiles_in = ts // TOKEN_TILE
    n_tot = tot.shape[1]

    def fwd_tile(b, i):
        return b * nt + i

    def bwd_tile(b, i):
        return b * nt + nt - 1 - i

    def token_specs(tile_of):
        rows = lambda b, i: (tile_of(b, i), 0)
        return [pl.BlockSpec((ts, B_QK_COLS), rows)] * 3 + [
            pl.BlockSpec((ts, B_V_COLS), rows),
            pl.BlockSpec((tiles_in, n_tot, B_QK_COLS), lambda b, i: (tile_of(b, i), 0, 0))]

    return pl.pallas_call(
        _gla_scan_kernel,
        grid=(batch, nt),
        in_specs=token_specs(fwd_tile) + token_specs(bwd_tile),
        out_specs=[pl.BlockSpec((ts, B_V_COLS), lambda b, i: (fwd_tile(b, i), 0)),
                   pl.BlockSpec((ts, B_V_COLS), lambda b, i: (bwd_tile(b, i), 0))],
        out_shape=[jax.ShapeDtypeStruct((n, B_V_COLS), BF16), jax.ShapeDtypeStruct((n, B_V_COLS), BF16)],
        scratch_shapes=[
            pltpu.VMEM((B_QK_COLS, tiles_in * n_tot), F32),
            pltpu.VMEM((B_HEADS, B_KEY_DIM, B_VAL_DIM), F32),
            pltpu.VMEM((B_HEADS, B_KEY_DIM, B_VAL_DIM), F32),
        ],
        compiler_params=_compiler_params(("parallel", "arbitrary")),
        name="gla_scan",
    )(*decayed_f, v, tot, *decayed_b, v, tot)


def _row(v):
    return v.reshape(1, -1).astype(F32)


def _pad_gate_weight(w2, first_row):
    out = jnp.zeros((LANES, w2.shape[1]), BF16)
    return lax.dynamic_update_slice(out, w2.astype(BF16), (first_row, 0))


def kernel(x, positions, attn_w_in, attn_sink, attn_w_out, gla_w_in, gla_gate_w2_fwd, gla_gate_b_fwd,
           gla_gate_w2_bwd, gla_gate_b_bwd, gla_norm_g, gla_w_out, mix_ln_g, mix_ln_b, mlp_w1, mlp_w2,
           mlp_ln_g, mlp_ln_b):
    batch, seq_len, d_model = x.shape
    n = batch * seq_len
    assert d_model == D_MODEL
    assert seq_len % ATTN_Q_TILE == 0 and seq_len % GLA_TILE == 0 and n % ATTN_POST_TILE == 0
    x2 = x.reshape(n, d_model)

    inv_freq = ROPE_THETA ** (-jnp.arange(0, A_HEAD_DIM, 2, dtype=F32) / A_HEAD_DIM)
    ang_t = inv_freq[:, None] * positions.astype(F32).reshape(1, n)
    q_cols = A_HEADS * A_HEAD_DIM
    k_cols = A_KV_HEADS * A_HEAD_DIM

    for i in range(DEPTH):
        j = i // 2
        w1 = mlp_w1[i].astype(BF16)
        w2 = mlp_w2[i].astype(BF16)
        tail = (_row(mix_ln_g[i]), _row(mix_ln_b[i]), w1, w2, _row(mlp_ln_g[i]), _row(mlp_ln_b[i]))
        if i % 2 == 0:
            w_in = attn_w_in[j].astype(BF16)
            wqv_t = jnp.concatenate([w_in[:, :q_cols], w_in[:, q_cols + k_cols:]], axis=1).T
            qt, kd, vt = _attn_in_proj(x2, ang_t, wqv_t, w_in[:, q_cols:q_cols + k_cols])
            o = _window_attention(qt, kd, vt, attn_sink[j].astype(F32), batch, seq_len)
            x2 = _post_mixer(_attn_post_kernel, ATTN_POST_TILE, [o], [], x2, attn_w_out[j].astype(BF16), *tail)
        else:
            w_in = gla_w_in[j]
            lr_cols = 2 * B_GATE_RANK
            w_in = jnp.pad(w_in, ((0, 0), (0, LANES - lr_cols))).astype(BF16)
            v, r, *decayed, tot = _gla_in_proj(
                x2, w_in, _pad_gate_weight(gla_gate_w2_fwd[j], 0), _row(gla_gate_b_fwd[j]),
                _pad_gate_weight(gla_gate_w2_bwd[j], B_GATE_RANK), _row(gla_gate_b_bwd[j]))
            o_f, o_b = _gla_scan(v, decayed[:3], decayed[3:], tot, batch, seq_len)
            x2 = _post_mixer(_gla_post_kernel, TOKEN_TILE, [o_f, o_b, r], [_row(gla_norm_g[j])], x2,
                             gla_w_out[j].astype(BF16), *tail)
    return x2.reshape(batch, seq_len, d_model)
```

```python
import jax
import jax.numpy as jnp
from jax import lax
from jax.experimental import pallas as pl
from jax.experimental.pallas import tpu as pltpu

D_MODEL = 1024
DEPTH = 2

A_HEADS = 16
A_KV_HEADS = 4
A_HEAD_DIM = D_MODEL // A_HEADS
A_GROUP = A_HEADS // A_KV_HEADS
WINDOW = 128
ROPE_THETA = 10000.0

B_HEADS = 4
B_KEY_DIM = (D_MODEL // 2) // B_HEADS
B_VAL_DIM = D_MODEL // B_HEADS
B_GATE_RANK = 16
B_GATE_TAU = 16.0
B_CHUNK = 64
B_QK_COLS = B_HEADS * B_KEY_DIM
B_V_COLS = B_HEADS * B_VAL_DIM

D_FF = 4 * D_MODEL
DN_ALPHA = float((2 * DEPTH) ** 0.25)
LN_EPS = 1e-5
HEAD_NORM_EPS = 1e-6
LOG2_E = 1.4426950408889634

LANES = 128
BF16_ROWS = 16
V_ROWS = A_HEAD_DIM + BF16_ROWS
VMEM_LIMIT_BYTES = 56 * 1024 * 1024

TOKEN_TILE = 512
ATTN_IN_TILE = 2048
ATTN_Q_TILE = 2048
GLA_TILE = 2 * TOKEN_TILE
FF_CHUNK = 1024
POST_PART_ROWS = 256
ATTN_POST_TILE = TOKEN_TILE

BF16 = jnp.bfloat16
F32 = jnp.float32


def _const_spec(shape):
    zeros = (0,) * len(shape)
    return pl.BlockSpec(shape, lambda *_: zeros, pipeline_mode=pl.Buffered(1))


def _compiler_params(semantics):
    return pltpu.CompilerParams(dimension_semantics=semantics, vmem_limit_bytes=VMEM_LIMIT_BYTES)


def _layer_norm(z, g, b):
    mu = jnp.mean(z, axis=-1, keepdims=True)
    zc = z - mu
    var = jnp.mean(zc * zc, axis=-1, keepdims=True)
    return zc * lax.rsqrt(var + LN_EPS) * g + b


def _lane_index(shape):
    return lax.broadcasted_iota(jnp.int32, shape, len(shape) - 1)


def _attn_in_kernel(x_ref, ang_ref, wqv_ref, wk_ref, qt_ref, k_ref, vt_ref):
    xb = x_ref[...].astype(BF16)
    half = A_HEAD_DIM // 2
    q_cols = A_HEADS * A_HEAD_DIM
    kv_cols = A_KV_HEADS * A_HEAD_DIM
    scale = A_HEAD_DIM ** -0.5 * LOG2_E
    block = 4 * A_HEAD_DIM

    def project_t(lo):
        return lax.dot_general(wqv_ref[lo:lo + block, :], xb, (((1,), (1,)), ((), ())),
                               preferred_element_type=F32)

    def rotate_q(hq, lo):
        for h in range(block // A_HEAD_DIM):
            r0 = h * A_HEAD_DIM
            t1 = hq[r0:r0 + half, :]
            t2 = hq[r0 + half:r0 + A_HEAD_DIM, :]
            qt_ref[lo + r0:lo + r0 + half, :] = ((t1 * cos_t - t2 * sin_t) * scale).astype(BF16)
            qt_ref[lo + r0 + half:lo + r0 + A_HEAD_DIM, :] = ((t2 * cos_t + t1 * sin_t) * scale).astype(BF16)

    def rotate_k(hk):
        cos = jnp.concatenate([cos_t] * (LANES // half), axis=0).T
        sin = jnp.concatenate([sin_t] * (LANES // half), axis=0).T
        lane = _lane_index(cos.shape)
        first_half = (lane % A_HEAD_DIM) < half
        sin_signed = jnp.where(first_half, -sin, sin)
        low_head = lane < A_HEAD_DIM
        for j in range(kv_cols // LANES):
            t = hk[:, j * LANES:(j + 1) * LANES]
            partner = jnp.where(first_half, pltpu.roll(t, LANES - half, 1), pltpu.roll(t, half, 1))
            t = t * cos + partner * sin_signed
            swapped = pltpu.roll(t, A_HEAD_DIM, 1)
            k_ref[:, (2 * j) * LANES:(2 * j + 1) * LANES] = jnp.where(low_head, t, swapped).astype(BF16)
            k_ref[:, (2 * j + 1) * LANES:(2 * j + 2) * LANES] = jnp.where(low_head, swapped, t).astype(BF16)

    hk = jnp.dot(xb, wk_ref[...], preferred_element_type=F32)
    cos_t = jnp.cos(ang_ref[...])
    sin_t = jnp.sin(ang_ref[...])
    pieces = {0: project_t(0)}
    rotate_k(hk)
    for p in range(1, q_cols // block):
        pieces[p] = project_t(p * block)
        rotate_q(pieces.pop(p - 1), (p - 1) * block)
    hv = project_t(q_cols)
    rotate_q(pieces.pop(q_cols // block - 1), q_cols - block)
    vt_ref[...] = hv.astype(BF16)


def _attn_in_proj(x2, ang_t, wqv_t, wk):
    n = x2.shape[0]
    tm = ATTN_IN_TILE
    q_cols = A_HEADS * A_HEAD_DIM
    kv_cols = A_KV_HEADS * A_HEAD_DIM
    k_dup = A_KV_HEADS * LANES
    return pl.pallas_call(
        _attn_in_kernel,
        grid=(n // tm,),
        in_specs=[
            pl.BlockSpec((tm, D_MODEL), lambda i: (i, 0)),
            pl.BlockSpec((ang_t.shape[0], tm), lambda i: (0, i)),
            _const_spec(wqv_t.shape),
            _const_spec(wk.shape),
        ],
        out_specs=[
            pl.BlockSpec((q_cols, tm), lambda i: (0, i)),
            pl.BlockSpec((tm, k_dup), lambda i: (i, 0)),
            pl.BlockSpec((kv_cols, tm), lambda i: (0, i)),
        ],
        out_shape=[
            jax.ShapeDtypeStruct((q_cols, n), BF16),
            jax.ShapeDtypeStruct((n, k_dup), BF16),
            jax.ShapeDtypeStruct((kv_cols, n), BF16),
        ],
        compiler_params=_compiler_params(("parallel",)),
        name="attn_in_proj",
    )(x2, ang_t, wqv_t, wk)


def _window_attn_kernel(sink_ref, qt_ref, kp_ref, km_ref, kn_ref, vp_ref, vm_ref, vn_ref, o_ref,
                        kbuf, vbuf):
    tq = ATTN_Q_TILE
    blk = WINDOW
    span = 3 * blk
    i = pl.program_id(1)
    last_tile = pl.num_programs(1) - 1
    kbuf[0:blk, :] = kp_ref[...]
    kbuf[blk:blk + tq, :] = km_ref[...]
    kbuf[blk + tq:, :] = kn_ref[...]
    for g in range(A_KV_HEADS):
        src = slice(g * A_HEAD_DIM, (g + 1) * A_HEAD_DIM)
        dst = slice(g * V_ROWS, g * V_ROWS + A_HEAD_DIM)
        vbuf[dst, 0:blk] = vp_ref[src, :]
        vbuf[dst, blk:blk + tq] = vm_ref[src, :]
        vbuf[dst, blk + tq:] = vn_ref[src, :]
        vbuf[g * V_ROWS + A_HEAD_DIM:(g + 1) * V_ROWS, :] = jnp.ones((BF16_ROWS, tq + 2 * blk), BF16)

    pair_shape = (blk, 2 * LANES)
    key = lax.broadcasted_iota(jnp.int32, pair_shape, 0)
    query = _lane_index(pair_shape) % blk
    neg_inf = jnp.float32(-jnp.inf)
    prev_bias = jnp.where(key >= query, 0.0, neg_inf)
    next_bias = jnp.where(key <= query, 0.0, neg_inf)
    first_prev_bias = jnp.where(i == 0, neg_inf, prev_bias)
    last_next_bias = jnp.where(i == last_tile, neg_inf, next_bias)
    low_rows = lax.broadcasted_iota(jnp.int32, (LANES, blk), 0) < A_HEAD_DIM
    zero = jnp.zeros((LANES, blk), BF16)

    n_blocks = tq // blk
    units = [(jj, g, pr) for jj in range(n_blocks) for g in range(A_KV_HEADS) for pr in range(A_GROUP // 2)]

    def pair_lanes(g, pr):
        return slice((g * 2 + pr) * LANES, (g * 2 + pr + 1) * LANES)

    def scores(jj, g, pr):
        kk = kbuf[jj * blk:jj * blk + span, g * LANES:(g + 1) * LANES]
        qt = qt_ref[pair_lanes(g, pr), jj * blk:(jj + 1) * blk]
        w = jnp.concatenate([jnp.where(low_rows, qt, zero), jnp.where(low_rows, zero, qt)], axis=1)
        return jnp.dot(kk, w, preferred_element_type=F32)

    def softmax(s, jj, g, pr):
        s_prev = s[0:blk] + (first_prev_bias if jj == 0 else prev_bias)
        s_mid = s[blk:2 * blk]
        s_next = s[2 * blk:] + (last_next_bias if jj == n_blocks - 1 else next_bias)
        head = g * A_GROUP + 2 * pr
        sink = jnp.concatenate([jnp.full((1, blk), sink_ref[head] * LOG2_E, F32),
                                jnp.full((1, blk), sink_ref[head + 1] * LOG2_E, F32)], axis=1)
        m = jnp.maximum(jnp.maximum(jnp.max(s_prev, axis=0, keepdims=True),
                                    jnp.max(s_mid, axis=0, keepdims=True)),
                        jnp.maximum(jnp.max(s_next, axis=0, keepdims=True), sink))
        p = jnp.concatenate([jnp.exp2(s_prev - m), jnp.exp2(s_mid - m), jnp.exp2(s_next - m)], axis=0)
        return p.astype(BF16), jnp.exp2(sink - m)

    def weighted_values(p, sink_term, jj, g, pr):
        vt = vbuf[g * V_ROWS:(g + 1) * V_ROWS, jj * blk:jj * blk + span]
        ot = jnp.dot(vt, p, preferred_element_type=F32)
        denom = ot[A_HEAD_DIM:A_HEAD_DIM + 1, :] + sink_term
        ot = ot[0:A_HEAD_DIM, :] * (1.0 / denom)
        both = jnp.concatenate([ot[:, 0:blk], ot[:, blk:]], axis=0)
        o_ref[jj * blk:(jj + 1) * blk, pair_lanes(g, pr)] = both.T.astype(BF16)

    s_stage = {}
    p_stage = {}
    width = 4
    n_steps = len(units) // width
    for n in range(n_steps + 2):
        for u in range(width):
            if n < n_steps:
                s_stage[n * width + u] = scores(*units[n * width + u])
        for u in range(width):
            if 0 <= n - 1 < n_steps:
                i1 = (n - 1) * width + u
                p_stage[i1] = softmax(s_stage.pop(i1), *units[i1])
        for u in range(width):
            if 0 <= n - 2 < n_steps:
                i2 = (n - 2) * width + u
                weighted_values(*p_stage.pop(i2), *units[i2])


def _window_attention(qt, kd, vt, sink, batch, seq_len):
    q_cols, n = qt.shape
    v_cols = vt.shape[0]
    tq = ATTN_Q_TILE
    blk = WINDOW
    nt = seq_len // tq
    per_tile = tq // blk
    nblk = seq_len // blk
    k_dup = kd.shape[1]

    def main_idx(b, i):
        return b * nt + i

    def prev_idx(b, i):
        return b * nblk + jnp.maximum(i * per_tile - 1, 0)

    def next_idx(b, i):
        return b * nblk + jnp.minimum((i + 1) * per_tile, nblk - 1)

    k_specs = [pl.BlockSpec((blk, k_dup), lambda b, i: (prev_idx(b, i), 0)),
               pl.BlockSpec((tq, k_dup), lambda b, i: (main_idx(b, i), 0)),
               pl.BlockSpec((blk, k_dup), lambda b, i: (next_idx(b, i), 0))]
    v_specs = [pl.BlockSpec((v_cols, blk), lambda b, i: (0, prev_idx(b, i))),
               pl.BlockSpec((v_cols, tq), lambda b, i: (0, main_idx(b, i))),
               pl.BlockSpec((v_cols, blk), lambda b, i: (0, next_idx(b, i)))]
    return pl.pallas_call(
        _window_attn_kernel,
        grid=(batch, nt),
        in_specs=[pl.BlockSpec(memory_space=pltpu.SMEM),
                  pl.BlockSpec((q_cols, tq), lambda b, i: (0, main_idx(b, i)))] + k_specs + v_specs,
        out_specs=pl.BlockSpec((tq, q_cols), lambda b, i: (main_idx(b, i), 0)),
        out_shape=jax.ShapeDtypeStruct((n, q_cols), BF16),
        scratch_shapes=[pltpu.VMEM((tq + 2 * blk, k_dup), BF16),
                        pltpu.VMEM((A_KV_HEADS * V_ROWS, tq + 2 * blk), BF16)],
        compiler_params=_compiler_params(("parallel", "parallel")),
        name="window_attn",
    )(sink, qt, kd, kd, kd, vt, vt, vt)


def _residual_mlp_tail(mixer_out, x_ref, wo_ref, ln1g_ref, ln1b_ref, w1_ref, w2_ref, ln2g_ref, ln2b_ref,
                       out_ref):
    tm = x_ref.shape[0]
    parts = [slice(lo, lo + POST_PART_ROWS) for lo in range(0, tm, POST_PART_ROWS)]
    n_ff = D_FF // FF_CHUNK

    def project(rows):
        return jnp.dot(mixer_out(rows), wo_ref[...], preferred_element_type=F32)

    def norm1(y, rows):
        x1 = _layer_norm(DN_ALPHA * x_ref[rows, :] + y, ln1g_ref[...], ln1b_ref[...])
        return x1, x1.astype(BF16)

    def mlp_chunk(x1b, acc, c):
        hc = jnp.dot(x1b, w1_ref[:, c * FF_CHUNK:(c + 1) * FF_CHUNK], preferred_element_type=F32)
        hc = jnp.square(jnp.maximum(hc, 0.0)).astype(BF16)
        part = jnp.dot(hc, w2_ref[c * FF_CHUNK:(c + 1) * FF_CHUNK, :], preferred_element_type=F32)
        return part if acc is None else acc + part

    def norm2(x1, acc, rows):
        out_ref[rows, :] = _layer_norm(DN_ALPHA * x1 + acc, ln2g_ref[...], ln2b_ref[...])

    n_parts = len(parts)
    normed = {0: norm1(project(parts[0]), parts[0])}
    done = None
    for k in range(n_parts):
        x1, x1b = normed.pop(k)
        if k + 1 < n_parts:
            y_next = project(parts[k + 1])
        acc = mlp_chunk(x1b, None, 0)
        if done is not None:
            norm2(*done)
        acc = mlp_chunk(x1b, acc, 1)
        if k + 1 < n_parts:
            normed[k + 1] = norm1(y_next, parts[k + 1])
        for c in range(2, n_ff):
            acc = mlp_chunk(x1b, acc, c)
        done = (x1, acc, parts[k])
    norm2(*done)


def _attn_post_kernel(o_ref, *tail_refs):
    _residual_mlp_tail(lambda rows: o_ref[rows, :], *tail_refs)


def _gla_post_kernel(of_ref, ob_ref, r_ref, ng_ref, *tail_refs):
    def mixer_out(rows):
        o = of_ref[rows, :].astype(F32) + ob_ref[rows, :].astype(F32)
        r = r_ref[rows, :]
        gate = r * (1.0 / (1.0 + jnp.exp(-r)))
        ng = ng_ref[...]
        heads = []
        for h in range(B_HEADS):
            oh = o[:, h * B_VAL_DIM:(h + 1) * B_VAL_DIM]
            oh = oh * lax.rsqrt(jnp.mean(oh * oh, axis=-1, keepdims=True) + HEAD_NORM_EPS)
            heads.append((oh * ng * gate[:, h * B_VAL_DIM:(h + 1) * B_VAL_DIM]).astype(BF16))
        return jnp.concatenate(heads, axis=-1)

    _residual_mlp_tail(mixer_out, *tail_refs)


def _post_mixer(kernel_fn, tm, mixer_inputs, extra_consts, x2, wo, ln1g, ln1b, w1, w2, ln2g, ln2b):
    n = x2.shape[0]
    tile = lambda cols: pl.BlockSpec((tm, cols), lambda i: (i, 0))
    consts = list(extra_consts)
    return pl.pallas_call(
        kernel_fn,
        grid=(n // tm,),
        in_specs=[tile(a.shape[1]) for a in mixer_inputs]
        + [_const_spec(c.shape) for c in consts]
        + [tile(D_MODEL), _const_spec(wo.shape), _const_spec(ln1g.shape), _const_spec(ln1b.shape),
           _const_spec(w1.shape), _const_spec(w2.shape), _const_spec(ln2g.shape), _const_spec(ln2b.shape)],
        out_specs=tile(D_MODEL),
        out_shape=jax.ShapeDtypeStruct((n, D_MODEL), F32),
        compiler_params=_compiler_params(("parallel",)),
        name=kernel_fn.__name__.strip("_"),
    )(*mixer_inputs, *consts, x2, wo, ln1g, ln1b, w1, w2, ln2g, ln2b)


def _gla_in_kernel(x_ref, w_ref, w2f_ref, gbf_ref, w2b_ref, gbb_ref,
                   v_ref, r_ref, qef_ref, kef_ref, kdf_ref, qeb_ref, keb_ref, kdb_ref, tot_ref):
    xb = x_ref[...].astype(BF16)
    tm = x_ref.shape[0]
    c1 = B_QK_COLS
    c2 = 2 * B_QK_COLS
    c3 = c2 + B_V_COLS
    c4 = c3 + B_V_COLS
    n_chunks = tm // B_CHUNK

    def project(lo, hi):
        return jnp.dot(xb, w_ref[:, lo:hi], preferred_element_type=F32)

    def chunk_totals(b, reverse):
        edge = 0 if reverse else B_CHUNK - 1
        return [b[c * B_CHUNK + edge:c * B_CHUNK + edge + 1, :] for c in range(n_chunks)]

    def decayed(q, k, b, totals, qe_ref, ke_ref, kd_ref):
        decay_total = jnp.concatenate(
            [jnp.broadcast_to(jnp.exp2(t), (B_CHUNK, t.shape[1])) for t in totals], axis=0)
        ke = k * jnp.exp2(-b)
        qe_ref[...] = (q * jnp.exp2(b)).astype(BF16)
        ke_ref[...] = ke.astype(BF16)
        kd_ref[...] = (ke * decay_total).astype(BF16)

    half_v = B_V_COLS // 2
    lr = project(c4, c4 + LANES).astype(BF16)
    q = project(0, c1) * (B_KEY_DIM ** -0.5)
    g_f = _log_gate(lr, w2f_ref[...], gbf_ref[...])
    k = project(c1, c2)
    g_b = _log_gate(lr, w2b_ref[...], gbb_ref[...])
    v_ref[:, :half_v] = project(c2, c2 + half_v).astype(BF16)
    b_f = _chunk_cumsum(g_f, False)
    totals_f = chunk_totals(b_f, False)
    v_ref[:, half_v:] = project(c2 + half_v, c3).astype(BF16)
    decayed(q, k, b_f, totals_f, qef_ref, kef_ref, kdf_ref)
    r_ref[:, :half_v] = project(c3, c3 + half_v)
    b_b = _chunk_cumsum(g_b, True)
    totals_b = chunk_totals(b_b, True)
    decayed(q, k, b_b, totals_b, qeb_ref, keb_ref, kdb_ref)
    r_ref[:, half_v:] = project(c3 + half_v, c4)
    tot_ref[0] = jnp.concatenate(totals_f + totals_b, axis=0)


def _gla_in_proj(x2, w, w2f, gbf, w2b, gbb):
    n = x2.shape[0]
    tm = TOKEN_TILE
    n_tot = 2 * (tm // B_CHUNK)
    tile = lambda cols: pl.BlockSpec((tm, cols), lambda i: (i, 0))
    qk = jax.ShapeDtypeStruct((n, B_QK_COLS), BF16)
    return pl.pallas_call(
        _gla_in_kernel,
        grid=(n // tm,),
        in_specs=[tile(D_MODEL), _const_spec(w.shape), _const_spec(w2f.shape), _const_spec(gbf.shape),
                  _const_spec(w2b.shape), _const_spec(gbb.shape)],
        out_specs=[tile(B_V_COLS), tile(B_V_COLS)] + [tile(B_QK_COLS)] * 6
        + [pl.BlockSpec((1, n_tot, B_QK_COLS), lambda i: (i, 0, 0))],
        out_shape=[jax.ShapeDtypeStruct((n, B_V_COLS), BF16), jax.ShapeDtypeStruct((n, B_V_COLS), F32)]
        + [qk] * 6 + [jax.ShapeDtypeStruct((n // tm, n_tot, B_QK_COLS), F32)],
        compiler_params=_compiler_params(("parallel",)),
        name="gla_in_proj",
    )(x2, w, w2f, gbf, w2b, gbb)


def _log_gate(lr, w2, bias):
    z = jnp.dot(lr, w2, preferred_element_type=F32) + bias
    log_term = jnp.log2(1.0 + jnp.exp(-jnp.abs(z)))
    return jnp.minimum(z, 0.0) * (LOG2_E / B_GATE_TAU) - log_term * (1.0 / B_GATE_TAU)


def _chunk_cumsum(g, reverse):
    hi = g.astype(BF16)
    rest = g - hi.astype(F32)
    mid = rest.astype(BF16)
    lo = (rest - mid.astype(F32)).astype(BF16)
    t_row = lax.broadcasted_iota(jnp.int32, (B_CHUNK, 3 * B_CHUNK), 0)
    t_col = lax.broadcasted_iota(jnp.int32, (B_CHUNK, 3 * B_CHUNK), 1) % B_CHUNK
    keep = (t_row <= t_col) if reverse else (t_row >= t_col)
    tri = jnp.where(keep, 1.0, 0.0).astype(BF16)
    out = []
    for c in range(g.shape[0] // B_CHUNK):
        rows = slice(c * B_CHUNK, (c + 1) * B_CHUNK)
        terms = jnp.concatenate([hi[rows], mid[rows], lo[rows]], axis=0)
        out.append(jnp.dot(tri, terms, preferred_element_type=F32))
    return jnp.concatenate(out, axis=0)


def _gla_scan_kernel(qef_ref, kef_ref, kdf_ref, vf_ref, totf_ref, qeb_ref, keb_ref, kdb_ref, vb_ref, totb_ref,
                     of_ref, ob_ref, d_ref, sf_ref, sb_ref):
    @pl.when(pl.program_id(1) == 0)
    def _():
        sf_ref[...] = jnp.zeros_like(sf_ref)
        sb_ref[...] = jnp.zeros_like(sb_ref)

    n_chunks = GLA_TILE // B_CHUNK
    pair_k = 2 * B_KEY_DIM
    sub_chunks = TOKEN_TILE // B_CHUNK
    totals = jnp.concatenate([totf_ref[t, 0:sub_chunks, :] for t in range(totf_ref.shape[0])]
                             + [totb_ref[t, sub_chunks:, :] for t in range(totb_ref.shape[0])], axis=0)
    d_ref[...] = jnp.exp2(totals).T

    directions = ((False, qef_ref, kef_ref, kdf_ref, vf_ref, 0, of_ref, sf_ref),
                  (True, qeb_ref, keb_ref, kdb_ref, vb_ref, n_chunks, ob_ref, sb_ref))
    t_row = lax.broadcasted_iota(jnp.int32, (B_CHUNK, 2 * B_CHUNK), 0)
    t_col = _lane_index((B_CHUNK, 2 * B_CHUNK)) % B_CHUNK
    first_head_lanes = _lane_index((B_CHUNK, pair_k)) < B_KEY_DIM
    zero_k = jnp.zeros((B_CHUNK, pair_k), BF16)
    zero_v = jnp.zeros((B_CHUNK, B_VAL_DIM), BF16)
    units = [(c, d, hp) for c in range(n_chunks) for d in range(2) for hp in range(B_HEADS // 2)]

    def chunk_rows(c, reverse):
        cc = n_chunks - 1 - c if reverse else c
        return cc, slice(cc * B_CHUNK, (cc + 1) * B_CHUNK)

    def prepare(c, d, hp):
        reverse, qe_ref, ke_ref, kd_ref, v_ref = directions[d][:5]
        _, rows = chunk_rows(c, reverse)
        cols = slice(hp * pair_k, (hp + 1) * pair_k)
        qe = qe_ref[rows, cols]
        ke = ke_ref[rows, cols]
        kd = kd_ref[rows, cols]
        ke_bd = jnp.concatenate([jnp.where(first_head_lanes, ke, zero_k),
                                 jnp.where(first_head_lanes, zero_k, ke)], axis=0)
        att = lax.dot_general(qe, ke_bd, (((1,), (1,)), ((), ())), preferred_element_type=F32)
        causal = (t_row <= t_col) if reverse else (t_row >= t_col)
        att = jnp.where(causal, att, 0.0).astype(BF16)
        updates = []
        for hh in range(2):
            h = 2 * hp + hh
            vh = v_ref[rows, h * B_VAL_DIM:(h + 1) * B_VAL_DIM]
            updates.append(lax.dot_general(kd[:, hh * B_KEY_DIM:(hh + 1) * B_KEY_DIM], vh,
                                           (((0,), (0,)), ((), ())), preferred_element_type=F32))
        return qe, att, updates

    def apply(prepared, c, d, hp):
        reverse, _, _, _, v_ref, d_col0, o_ref, s_ref = directions[d]
        qe, att, updates = prepared
        cc, rows = chunk_rows(c, reverse)
        for hh in range(2):
            h = 2 * hp + hh
            vs = slice(h * B_VAL_DIM, (h + 1) * B_VAL_DIM)
            vh = v_ref[rows, vs]
            state = s_ref[h]
            lhs = jnp.concatenate([qe[:, hh * B_KEY_DIM:(hh + 1) * B_KEY_DIM], att], axis=1)
            rhs = jnp.concatenate([state.astype(BF16)] + ([vh, zero_v] if hh == 0 else [zero_v, vh]), axis=0)
            o_ref[rows, vs] = jnp.dot(lhs, rhs, preferred_element_type=F32).astype(o_ref.dtype)
            decay = d_ref[h * B_KEY_DIM:(h + 1) * B_KEY_DIM, d_col0 + cc:d_col0 + cc + 1]
            s_ref[h] = state * decay + updates[hh]

    prepared = {}
    width = 2
    n_steps = len(units) // width
    for n in range(n_steps + 1):
        for u in range(width):
            if n < n_steps:
                prepared[n * width + u] = prepare(*units[n * width + u])
        for u in range(width):
            if n >= 1:
                i1 = (n - 1) * width + u
                apply(prepared.pop(i1), *units[i1])


def _gla_scan(v, decayed_f, decayed_b, tot, batch, seq_len):
    n = v.shape[0]
    ts = GLA_TILE
    nt = seq_len // ts
    tiles_in = ts // TOKEN_TILE
    n_tot = tot.shape[1]

    def fwd_tile(b, i):
        return b * nt + i

    def bwd_tile(b, i):
        return b * nt + nt - 1 - i

    def token_specs(tile_of):
        rows = lambda b, i: (tile_of(b, i), 0)
        return [pl.BlockSpec((ts, B_QK_COLS), rows)] * 3 + [
            pl.BlockSpec((ts, B_V_COLS), rows),
            pl.BlockSpec((tiles_in, n_tot, B_QK_COLS), lambda b, i: (tile_of(b, i), 0, 0))]

    return pl.pallas_call(
        _gla_scan_kernel,
        grid=(batch, nt),
        in_specs=token_specs(fwd_tile) + token_specs(bwd_tile),
        out_specs=[pl.BlockSpec((ts, B_V_COLS), lambda b, i: (fwd_tile(b, i), 0)),
                   pl.BlockSpec((ts, B_V_COLS), lambda b, i: (bwd_tile(b, i), 0))],
        out_shape=[jax.ShapeDtypeStruct((n, B_V_COLS), BF16), jax.ShapeDtypeStruct((n, B_V_COLS), BF16)],
        scratch_shapes=[
            pltpu.VMEM((B_QK_COLS, tiles_in * n_tot), F32),
            pltpu.VMEM((B_HEADS, B_KEY_DIM, B_VAL_DIM), F32),
            pltpu.VMEM((B_HEADS, B_KEY_DIM, B_VAL_DIM), F32),
        ],
        compiler_params=_compiler_params(("parallel", "arbitrary")),
        name="gla_scan",
    )(*decayed_f, v, tot, *decayed_b, v, tot)


def _row(v):
    return v.reshape(1, -1).astype(F32)


def _pad_gate_weight(w2, first_row):
    out = jnp.zeros((LANES, w2.shape[1]), BF16)
    return lax.dynamic_update_slice(out, w2.astype(BF16), (first_row, 0))


def kernel(x, positions, attn_w_in, attn_sink, attn_w_out, gla_w_in, gla_gate_w2_fwd, gla_gate_b_fwd,
           gla_gate_w2_bwd, gla_gate_b_bwd, gla_norm_g, gla_w_out, mix_ln_g, mix_ln_b, mlp_w1, mlp_w2,
           mlp_ln_g, mlp_ln_b):
    batch, seq_len, d_model = x.shape
    n = batch * seq_len
    assert d_model == D_MODEL
    assert seq_len % ATTN_Q_TILE == 0 and seq_len % GLA_TILE == 0 and n % ATTN_POST_TILE == 0
    x2 = x.reshape(n, d_model)

    inv_freq = ROPE_THETA ** (-jnp.arange(0, A_HEAD_DIM, 2, dtype=F32) / A_HEAD_DIM)
    ang_t = inv_freq[:, None] * positions.astype(F32).reshape(1, n)
    q_cols = A_HEADS * A_HEAD_DIM
    k_cols = A_KV_HEADS * A_HEAD_DIM

    for i in range(DEPTH):
        j = i // 2
        w1 = mlp_w1[i].astype(BF16)
        w2 = mlp_w2[i].astype(BF16)
        tail = (_row(mix_ln_g[i]), _row(mix_ln_b[i]), w1, w2, _row(mlp_ln_g[i]), _row(mlp_ln_b[i]))
        if i % 2 == 0:
            w_in = attn_w_in[j].astype(BF16)
            wqv_t = jnp.concatenate([w_in[:, :q_cols], w_in[:, q_cols + k_cols:]], axis=1).T
            qt, kd, vt = _attn_in_proj(x2, ang_t, wqv_t, w_in[:, q_cols:q_cols + k_cols])
            o = _window_attention(qt, kd, vt, attn_sink[j].astype(F32), batch, seq_len)
            x2 = _post_mixer(_attn_post_kernel, ATTN_POST_TILE, [o], [], x2, attn_w_out[j].astype(BF16), *tail)
        else:
            w_in = gla_w_in[j]
            lr_cols = 2 * B_GATE_RANK
            w_in = jnp.pad(w_in, ((0, 0), (0, LANES - lr_cols))).astype(BF16)
            v, r, *decayed, tot = _gla_in_proj(
                x2, w_in, _pad_gate_weight(gla_gate_w2_fwd[j], 0), _row(gla_gate_b_fwd[j]),
                _pad_gate_weight(gla_gate_w2_bwd[j], B_GATE_RANK), _row(gla_gate_b_bwd[j]))
            o_f, o_b = _gla_scan(v, decayed[:3], decayed[3:], tot, batch, seq_len)
            x2 = _post_mixer(_gla_post_kernel, TOKEN_TILE, [o_f, o_b, r], [_row(gla_norm_g[j])], x2,
                             gla_w_out[j].astype(BF16), *tail)
    return x2.reshape(batch, seq_len, d_model)
```

```python
import jax
import jax.numpy as jnp
from jax import lax
from jax.experimental import pallas as pl
from jax.experimental.pallas import tpu as pltpu

D_MODEL = 1024
DEPTH = 2

A_HEADS = 16
A_KV_HEADS = 4
A_HEAD_DIM = D_MODEL // A_HEADS
A_GROUP = A_HEADS // A_KV_HEADS
WINDOW = 128
ROPE_THETA = 10000.0

B_HEADS = 4
B_KEY_DIM = (D_MODEL // 2) // B_HEADS
B_VAL_DIM = D_MODEL // B_HEADS
B_GATE_RANK = 16
B_GATE_TAU = 16.0
B_CHUNK = 64
B_QK_COLS = B_HEADS * B_KEY_DIM
B_V_COLS = B_HEADS * B_VAL_DIM

D_FF = 4 * D_MODEL
DN_ALPHA = float((2 * DEPTH) ** 0.25)
LN_EPS = 1e-5
HEAD_NORM_EPS = 1e-6
LOG2_E = 1.4426950408889634

LANES = 128
BF16_ROWS = 16
V_ROWS = A_HEAD_DIM + BF16_ROWS
VMEM_LIMIT_BYTES = 56 * 1024 * 1024

TOKEN_TILE = 512
ATTN_IN_TILE = 2048
ATTN_Q_TILE = 2048
GLA_TILE = 2 * TOKEN_TILE
FF_CHUNK = 1024
POST_PART_ROWS = 256
ATTN_POST_TILE = TOKEN_TILE

BF16 = jnp.bfloat16
F32 = jnp.float32


def _const_spec(shape):
    zeros = (0,) * len(shape)
    return pl.BlockSpec(shape, lambda *_: zeros, pipeline_mode=pl.Buffered(1))


def _compiler_params(semantics):
    return pltpu.CompilerParams(dimension_semantics=semantics, vmem_limit_bytes=VMEM_LIMIT_BYTES)


def _layer_norm(z, g, b):
    mu = jnp.mean(z, axis=-1, keepdims=True)
    zc = z - mu
    var = jnp.mean(zc * zc, axis=-1, keepdims=True)
    return zc * lax.rsqrt(var + LN_EPS) * g + b


def _lane_index(shape):
    return lax.broadcasted_iota(jnp.int32, shape, len(shape) - 1)


def _attn_in_kernel(x_ref, ang_ref, wqv_ref, wk_ref, qt_ref, k_ref, vt_ref):
    xb = x_ref[...].astype(BF16)
    half = A_HEAD_DIM // 2
    q_cols = A_HEADS * A_HEAD_DIM
    kv_cols = A_KV_HEADS * A_HEAD_DIM
    scale = A_HEAD_DIM ** -0.5 * LOG2_E
    block = 4 * A_HEAD_DIM

    def project_t(lo):
        return lax.dot_general(wqv_ref[lo:lo + block, :], xb, (((1,), (1,)), ((), ())),
                               preferred_element_type=F32)

    def rotate_q(hq, lo):
        for h in range(block // A_HEAD_DIM):
            r0 = h * A_HEAD_DIM
            t1 = hq[r0:r0 + half, :]
            t2 = hq[r0 + half:r0 + A_HEAD_DIM, :]
            qt_ref[lo + r0:lo + r0 + half, :] = ((t1 * cos_t - t2 * sin_t) * scale).astype(BF16)
            qt_ref[lo + r0 + half:lo + r0 + A_HEAD_DIM, :] = ((t2 * cos_t + t1 * sin_t) * scale).astype(BF16)

    def rotate_k(hk):
        cos = jnp.concatenate([cos_t] * (LANES // half), axis=0).T
        sin = jnp.concatenate([sin_t] * (LANES // half), axis=0).T
        lane = _lane_index(cos.shape)
        first_half = (lane % A_HEAD_DIM) < half
        sin_signed = jnp.where(first_half, -sin, sin)
        low_head = lane < A_HEAD_DIM
        for j in range(kv_cols // LANES):
            t = hk[:, j * LANES:(j + 1) * LANES]
            partner = jnp.where(first_half, pltpu.roll(t, LANES - half, 1), pltpu.roll(t, half, 1))
            t = t * cos + partner * sin_signed
            swapped = pltpu.roll(t, A_HEAD_DIM, 1)
            k_ref[:, (2 * j) * LANES:(2 * j + 1) * LANES] = jnp.where(low_head, t, swapped).astype(BF16)
            k_ref[:, (2 * j + 1) * LANES:(2 * j + 2) * LANES] = jnp.where(low_head, swapped, t).astype(BF16)

    hk = jnp.dot(xb, wk_ref[...], preferred_element_type=F32)
    cos_t = jnp.cos(ang_ref[...])
    sin_t = jnp.sin(ang_ref[...])
    pieces = {0: project_t(0)}
    rotate_k(hk)
    for p in range(1, q_cols // block):
        pieces[p] = project_t(p * block)
        rotate_q(pieces.pop(p - 1), (p - 1) * block)
    hv = project_t(q_cols)
    rotate_q(pieces.pop(q_cols // block - 1), q_cols - block)
    vt_ref[...] = hv.astype(BF16)


def _attn_in_proj(x2, ang_t, wqv_t, wk):
    n = x2.shape[0]
    tm = ATTN_IN_TILE
    q_cols = A_HEADS * A_HEAD_DIM
    kv_cols = A_KV_HEADS * A_HEAD_DIM
    k_dup = A_KV_HEADS * LANES
    return pl.pallas_call(
        _attn_in_kernel,
        grid=(n // tm,),
        in_specs=[
            pl.BlockSpec((tm, D_MODEL), lambda i: (i, 0)),
            pl.BlockSpec((ang_t.shape[0], tm), lambda i: (0, i)),
            _const_spec(wqv_t.shape),
            _const_spec(wk.shape),
        ],
        out_specs=[
            pl.BlockSpec((q_cols, tm), lambda i: (0, i)),
            pl.BlockSpec((tm, k_dup), lambda i: (i, 0)),
            pl.BlockSpec((kv_cols, tm), lambda i: (0, i)),
        ],
        out_shape=[
            jax.ShapeDtypeStruct((q_cols, n), BF16),
            jax.ShapeDtypeStruct((n, k_dup), BF16),
            jax.ShapeDtypeStruct((kv_cols, n), BF16),
        ],
        compiler_params=_compiler_params(("parallel",)),
        name="attn_in_proj",
    )(x2, ang_t, wqv_t, wk)


def _window_attn_kernel(sink_ref, qt_ref, kp_ref, km_ref, kn_ref, vp_ref, vm_ref, vn_ref, o_ref,
                        kbuf, vbuf):
    tq = ATTN_Q_TILE
    blk = WINDOW
    span = 3 * blk
    i = pl.program_id(1)
    last_tile = pl.num_programs(1) - 1
    kbuf[0:blk, :] = kp_ref[...]
    kbuf[blk:blk + tq, :] = km_ref[...]
    kbuf[blk + tq:, :] = kn_ref[...]
    for g in range(A_KV_HEADS):
        src = slice(g * A_HEAD_DIM, (g + 1) * A_HEAD_DIM)
        dst = slice(g * V_ROWS, g * V_ROWS + A_HEAD_DIM)
        vbuf[dst, 0:blk] = vp_ref[src, :]
        vbuf[dst, blk:blk + tq] = vm_ref[src, :]
        vbuf[dst, blk + tq:] = vn_ref[src, :]
        vbuf[g * V_ROWS + A_HEAD_DIM:(g + 1) * V_ROWS, :] = jnp.ones((BF16_ROWS, tq + 2 * blk), BF16)

    pair_shape = (blk, 2 * LANES)
    key = lax.broadcasted_iota(jnp.int32, pair_shape, 0)
    query = _lane_index(pair_shape) % blk
    neg_inf = jnp.float32(-jnp.inf)
    prev_bias = jnp.where(key >= query, 0.0, neg_inf)
    next_bias = jnp.where(key <= query, 0.0, neg_inf)
    first_prev_bias = jnp.where(i == 0, neg_inf, prev_bias)
    last_next_bias = jnp.where(i == last_tile, neg_inf, next_bias)
    low_rows = lax.broadcasted_iota(jnp.int32, (LANES, blk), 0) < A_HEAD_DIM
    zero = jnp.zeros((LANES, blk), BF16)

    n_blocks = tq // blk
    units = [(jj, g, pr) for jj in range(n_blocks) for g in range(A_KV_HEADS) for pr in range(A_GROUP // 2)]

    def pair_lanes(g, pr):
        return slice((g * 2 + pr) * LANES, (g * 2 + pr + 1) * LANES)

    def scores(jj, g, pr):
        kk = kbuf[jj * blk:jj * blk + span, g * LANES:(g + 1) * LANES]
        qt = qt_ref[pair_lanes(g, pr), jj * blk:(jj + 1) * blk]
        w = jnp.concatenate([jnp.where(low_rows, qt, zero), jnp.where(low_rows, zero, qt)], axis=1)
        return jnp.dot(kk, w, preferred_element_type=F32)

    def softmax(s, jj, g, pr):
        s_prev = s[0:blk] + (first_prev_bias if jj == 0 else prev_bias)
        s_mid = s[blk:2 * blk]
        s_next = s[2 * blk:] + (last_next_bias if jj == n_blocks - 1 else next_bias)
        head = g * A_GROUP + 2 * pr
        sink = jnp.concatenate([jnp.full((1, blk), sink_ref[head] * LOG2_E, F32),
                                jnp.full((1, blk), sink_ref[head + 1] * LOG2_E, F32)], axis=1)
        m = jnp.maximum(jnp.maximum(jnp.max(s_prev, axis=0, keepdims=True),
                                    jnp.max(s_mid, axis=0, keepdims=True)),
                        jnp.maximum(jnp.max(s_next, axis=0, keepdims=True), sink))
        p = jnp.concatenate([jnp.exp2(s_prev - m), jnp.exp2(s_mid - m), jnp.exp2(s_next - m)], axis=0)
        return p.astype(BF16), jnp.exp2(sink - m)

    def weighted_values(p, sink_term, jj, g, pr):
        vt = vbuf[g * V_ROWS:(g + 1) * V_ROWS, jj * blk:jj * blk + span]
        ot = jnp.dot(vt, p, preferred_element_type=F32)
        denom = ot[A_HEAD_DIM:A_HEAD_DIM + 1, :] + sink_term
        ot = ot[0:A_HEAD_DIM, :] * (1.0 / denom)
        both = jnp.concatenate([ot[:, 0:blk], ot[:, blk:]], axis=0)
        o_ref[jj * blk:(jj + 1) * blk, pair_lanes(g, pr)] = both.T.astype(BF16)

    s_stage = {}
    p_stage = {}
    width = 4
    n_steps = len(units) // width
    for n in range(n_steps + 2):
        for u in range(width):
            if n < n_steps:
                s_stage[n * width + u] = scores(*units[n * width + u])
        for u in range(width):
            if 0 <= n - 1 < n_steps:
                i1 = (n - 1) * width + u
                p_stage[i1] = softmax(s_stage.pop(i1), *units[i1])
        for u in range(width):
            if 0 <= n - 2 < n_steps:
                i2 = (n - 2) * width + u
                weighted_values(*p_stage.pop(i2), *units[i2])


def _window_attention(qt, kd, vt, sink, batch, seq_len):
    q_cols, n = qt.shape
    v_cols = vt.shape[0]
    tq = ATTN_Q_TILE
    blk = WINDOW
    nt = seq_len // tq
    per_tile = tq // blk
    nblk = seq_len // blk
    k_dup = kd.shape[1]

    def main_idx(b, i):
        return b * nt + i

    def prev_idx(b, i):
        return b * nblk + jnp.maximum(i * per_tile - 1, 0)

    def next_idx(b, i):
        return b * nblk + jnp.minimum((i + 1) * per_tile, nblk - 1)

    k_specs = [pl.BlockSpec((blk, k_dup), lambda b, i: (prev_idx(b, i), 0)),
               pl.BlockSpec((tq, k_dup), lambda b, i: (main_idx(b, i), 0)),
               pl.BlockSpec((blk, k_dup), lambda b, i: (next_idx(b, i), 0))]
    v_specs = [pl.BlockSpec((v_cols, blk), lambda b, i: (0, prev_idx(b, i))),
               pl.BlockSpec((v_cols, tq), lambda b, i: (0, main_idx(b, i))),
               pl.BlockSpec((v_cols, blk), lambda b, i: (0, next_idx(b, i)))]
    return pl.pallas_call(
        _window_attn_kernel,
        grid=(batch, nt),
        in_specs=[pl.BlockSpec(memory_space=pltpu.SMEM),
                  pl.BlockSpec((q_cols, tq), lambda b, i: (0, main_idx(b, i)))] + k_specs + v_specs,
        out_specs=pl.BlockSpec((tq, q_cols), lambda b, i: (main_idx(b, i), 0)),
        out_shape=jax.ShapeDtypeStruct((n, q_cols), BF16),
        scratch_shapes=[pltpu.VMEM((tq + 2 * blk, k_dup), BF16),
                        pltpu.VMEM((A_KV_HEADS * V_ROWS, tq + 2 * blk), BF16)],
        compiler_params=_compiler_params(("parallel", "parallel")),
        name="window_attn",
    )(sink, qt, kd, kd, kd, vt, vt, vt)


def _residual_mlp_tail(mixer_out, x_ref, wo_ref, ln1g_ref, ln1b_ref, w1_ref, w2_ref, ln2g_ref, ln2b_ref,
                       out_ref):
    tm = x_ref.shape[0]
    parts = [slice(lo, lo + POST_PART_ROWS) for lo in range(0, tm, POST_PART_ROWS)]
    n_ff = D_FF // FF_CHUNK

    def project(rows):
        return jnp.dot(mixer_out(rows), wo_ref[...], preferred_element_type=F32)

    def norm1(y, rows):
        x1 = _layer_norm(DN_ALPHA * x_ref[rows, :] + y, ln1g_ref[...], ln1b_ref[...])
        return x1, x1.astype(BF16)

    def mlp_chunk(x1b, acc, c):
        hc = jnp.dot(x1b, w1_ref[:, c * FF_CHUNK:(c + 1) * FF_CHUNK], preferred_element_type=F32)
        hc = jnp.square(jnp.maximum(hc, 0.0)).astype(BF16)
        part = jnp.dot(hc, w2_ref[c * FF_CHUNK:(c + 1) * FF_CHUNK, :], preferred_element_type=F32)
        return part if acc is None else acc + part

    def norm2(x1, acc, rows):
        out_ref[rows, :] = _layer_norm(DN_ALPHA * x1 + acc, ln2g_ref[...], ln2b_ref[...])

    n_parts = len(parts)
    normed = {0: norm1(project(parts[0]), parts[0])}
    done = None
    for k in range(n_parts):
        x1, x1b = normed.pop(k)
        if k + 1 < n_parts:
            y_next = project(parts[k + 1])
        acc = mlp_chunk(x1b, None, 0)
        if done is not None:
            norm2(*done)
        acc = mlp_chunk(x1b, acc, 1)
        if k + 1 < n_parts:
            normed[k + 1] = norm1(y_next, parts[k + 1])
        for c in range(2, n_ff):
            acc = mlp_chunk(x1b, acc, c)
        done = (x1, acc, parts[k])
    norm2(*done)


def _attn_post_kernel(o_ref, *tail_refs):
    _residual_mlp_tail(lambda rows: o_ref[rows, :], *tail_refs)


def _gla_post_kernel(of_ref, ob_ref, r_ref, ng_ref, *tail_refs):
    def mixer_out(rows):
        o = of_ref[rows, :].astype(F32) + ob_ref[rows, :].astype(F32)
        r = r_ref[rows, :]
        gate = r * (1.0 / (1.0 + jnp.exp(-r)))
        ng = ng_ref[...]
        heads = []
        for h in range(B_HEADS):
            oh = o[:, h * B_VAL_DIM:(h + 1) * B_VAL_DIM]
            oh = oh * lax.rsqrt(jnp.mean(oh * oh, axis=-1, keepdims=True) + HEAD_NORM_EPS)
            heads.append((oh * ng * gate[:, h * B_VAL_DIM:(h + 1) * B_VAL_DIM]).astype(BF16))
        return jnp.concatenate(heads, axis=-1)

    _residual_mlp_tail(mixer_out, *tail_refs)


def _post_mixer(kernel_fn, tm, mixer_inputs, extra_consts, x2, wo, ln1g, ln1b, w1, w2, ln2g, ln2b):
    n = x2.shape[0]
    tile = lambda cols: pl.BlockSpec((tm, cols), lambda i: (i, 0))
    consts = list(extra_consts)
    return pl.pallas_call(
        kernel_fn,
        grid=(n // tm,),
        in_specs=[tile(a.shape[1]) for a in mixer_inputs]
        + [_const_spec(c.shape) for c in consts]
        + [tile(D_MODEL), _const_spec(wo.shape), _const_spec(ln1g.shape), _const_spec(ln1b.shape),
           _const_spec(w1.shape), _const_spec(w2.shape), _const_spec(ln2g.shape), _const_spec(ln2b.shape)],
        out_specs=tile(D_MODEL),
        out_shape=jax.ShapeDtypeStruct((n, D_MODEL), F32),
        compiler_params=_compiler_params(("parallel",)),
        name=kernel_fn.__name__.strip("_"),
    )(*mixer_inputs, *consts, x2, wo, ln1g, ln1b, w1, w2, ln2g, ln2b)


def _gla_in_kernel(x_ref, w_ref, w2f_ref, gbf_ref, w2b_ref, gbb_ref,
                   v_ref, r_ref, qef_ref, kef_ref, kdf_ref, qeb_ref, keb_ref, kdb_ref, tot_ref):
    xb = x_ref[...].astype(BF16)
    tm = x_ref.shape[0]
    c0 = LANES
    c1 = c0 + B_QK_COLS
    c2 = c1 + B_QK_COLS
    c3 = c2 + B_V_COLS
    c4 = c3 + B_V_COLS
    n_chunks = tm // B_CHUNK

    def project(lo, hi):
        return jnp.dot(xb, w_ref[:, lo:hi], preferred_element_type=F32)

    def chunk_totals(b, reverse):
        edge = 0 if reverse else B_CHUNK - 1
        return [b[c * B_CHUNK + edge:c * B_CHUNK + edge + 1, :] for c in range(n_chunks)]

    def decayed(q, k, b, totals, qe_ref, ke_ref, kd_ref):
        decay_total = jnp.concatenate(
            [jnp.broadcast_to(jnp.exp2(t), (B_CHUNK, t.shape[1])) for t in totals], axis=0)
        ke = k * jnp.exp2(-b)
        qe_ref[...] = (q * jnp.exp2(b)).astype(BF16)
        ke_ref[...] = ke.astype(BF16)
        kd_ref[...] = (ke * decay_total).astype(BF16)

    half_v = B_V_COLS // 2
    lr_q = project(0, c1)
    lr = lr_q[:, :c0].astype(BF16)
    q = lr_q[:, c0:] * (B_KEY_DIM ** -0.5)
    g_f = _log_gate(lr, w2f_ref[...], gbf_ref[...])
    k = project(c1, c2)
    g_b = _log_gate(lr, w2b_ref[...], gbb_ref[...])
    v_ref[:, :half_v] = project(c2, c2 + half_v).astype(BF16)
    b_f = _chunk_cumsum(g_f, False)
    totals_f = chunk_totals(b_f, False)
    v_ref[:, half_v:] = project(c2 + half_v, c3).astype(BF16)
    decayed(q, k, b_f, totals_f, qef_ref, kef_ref, kdf_ref)
    r_ref[:, :half_v] = project(c3, c3 + half_v)
    b_b = _chunk_cumsum(g_b, True)
    totals_b = chunk_totals(b_b, True)
    decayed(q, k, b_b, totals_b, qeb_ref, keb_ref, kdb_ref)
    r_ref[:, half_v:] = project(c3 + half_v, c4)
    tot_ref[0] = jnp.concatenate(totals_f + totals_b, axis=0)


def _gla_in_proj(x2, w, w2f, gbf, w2b, gbb):
    n = x2.shape[0]
    tm = TOKEN_TILE
    n_tot = 2 * (tm // B_CHUNK)
    tile = lambda cols: pl.BlockSpec((tm, cols), lambda i: (i, 0))
    qk = jax.ShapeDtypeStruct((n, B_QK_COLS), BF16)
    return pl.pallas_call(
        _gla_in_kernel,
        grid=(n // tm,),
        in_specs=[tile(D_MODEL), _const_spec(w.shape), _const_spec(w2f.shape), _const_spec(gbf.shape),
                  _const_spec(w2b.shape), _const_spec(gbb.shape)],
        out_specs=[tile(B_V_COLS), tile(B_V_COLS)] + [tile(B_QK_COLS)] * 6
        + [pl.BlockSpec((1, n_tot, B_QK_COLS), lambda i: (i, 0, 0))],
        out_shape=[jax.ShapeDtypeStruct((n, B_V_COLS), BF16), jax.ShapeDtypeStruct((n, B_V_COLS), F32)]
        + [qk] * 6 + [jax.ShapeDtypeStruct((n // tm, n_tot, B_QK_COLS), F32)],
        compiler_params=_compiler_params(("parallel",)),
        name="gla_in_proj",
    )(x2, w, w2f, gbf, w2b, gbb)


def _log_gate(lr, w2, bias):
    z = jnp.dot(lr, w2, preferred_element_type=F32) + bias
    log_term = jnp.log2(1.0 + jnp.exp(-jnp.abs(z)))
    return jnp.minimum(z, 0.0) * (LOG2_E / B_GATE_TAU) - log_term * (1.0 / B_GATE_TAU)


def _chunk_cumsum(g, reverse):
    hi = g.astype(BF16)
    rest = g - hi.astype(F32)
    mid = rest.astype(BF16)
    lo = (rest - mid.astype(F32)).astype(BF16)
    t_row = lax.broadcasted_iota(jnp.int32, (B_CHUNK, 3 * B_CHUNK), 0)
    t_col = lax.broadcasted_iota(jnp.int32, (B_CHUNK, 3 * B_CHUNK), 1) % B_CHUNK
    keep = (t_row <= t_col) if reverse else (t_row >= t_col)
    tri = jnp.where(keep, 1.0, 0.0).astype(BF16)
    out = []
    for c in range(g.shape[0] // B_CHUNK):
        rows = slice(c * B_CHUNK, (c + 1) * B_CHUNK)
        terms = jnp.concatenate([hi[rows], mid[rows], lo[rows]], axis=0)
        out.append(jnp.dot(tri, terms, preferred_element_type=F32))
    return jnp.concatenate(out, axis=0)


def _gla_scan_kernel(qef_ref, kef_ref, kdf_ref, vf_ref, totf_ref, qeb_ref, keb_ref, kdb_ref, vb_ref, totb_ref,
                     of_ref, ob_ref, d_ref, sf_ref, sb_ref):
    @pl.when(pl.program_id(1) == 0)
    def _():
        sf_ref[...] = jnp.zeros_like(sf_ref)
        sb_ref[...] = jnp.zeros_like(sb_ref)

    n_chunks = GLA_TILE // B_CHUNK
    pair_k = 2 * B_KEY_DIM
    sub_chunks = TOKEN_TILE // B_CHUNK
    totals = jnp.concatenate([totf_ref[t, 0:sub_chunks, :] for t in range(totf_ref.shape[0])]
                             + [totb_ref[t, sub_chunks:, :] for t in range(totb_ref.shape[0])], axis=0)
    d_ref[...] = jnp.exp2(totals).T

    directions = ((False, qef_ref, kef_ref, kdf_ref, vf_ref, 0, of_ref, sf_ref),
                  (True, qeb_ref, keb_ref, kdb_ref, vb_ref, n_chunks, ob_ref, sb_ref))
    t_row = lax.broadcasted_iota(jnp.int32, (B_CHUNK, 2 * B_CHUNK), 0)
    t_col = _lane_index((B_CHUNK, 2 * B_CHUNK)) % B_CHUNK
    first_head_lanes = _lane_index((B_CHUNK, pair_k)) < B_KEY_DIM
    zero_k = jnp.zeros((B_CHUNK, pair_k), BF16)
    zero_v = jnp.zeros((B_CHUNK, B_VAL_DIM), BF16)
    units = [(c, d, hp) for c in range(n_chunks) for d in range(2) for hp in range(B_HEADS // 2)]

    def chunk_rows(c, reverse):
        cc = n_chunks - 1 - c if reverse else c
        return cc, slice(cc * B_CHUNK, (cc + 1) * B_CHUNK)

    def prepare(c, d, hp):
        reverse, qe_ref, ke_ref, kd_ref, v_ref = directions[d][:5]
        _, rows = chunk_rows(c, reverse)
        cols = slice(hp * pair_k, (hp + 1) * pair_k)
        qe = qe_ref[rows, cols]
        ke = ke_ref[rows, cols]
        kd = kd_ref[rows, cols]
        ke_bd = jnp.concatenate([jnp.where(first_head_lanes, ke, zero_k),
                                 jnp.where(first_head_lanes, zero_k, ke)], axis=0)
        att = lax.dot_general(qe, ke_bd, (((1,), (1,)), ((), ())), preferred_element_type=F32)
        causal = (t_row <= t_col) if reverse else (t_row >= t_col)
        att = jnp.where(causal, att, 0.0).astype(BF16)
        updates = []
        for hh in range(2):
            h = 2 * hp + hh
            vh = v_ref[rows, h * B_VAL_DIM:(h + 1) * B_VAL_DIM]
            updates.append(lax.dot_general(kd[:, hh * B_KEY_DIM:(hh + 1) * B_KEY_DIM], vh,
                                           (((0,), (0,)), ((), ())), preferred_element_type=F32))
        return qe, att, updates

    def apply(prepared, c, d, hp):
        reverse, _, _, _, v_ref, d_col0, o_ref, s_ref = directions[d]
        qe, att, updates = prepared
        cc, rows = chunk_rows(c, reverse)
        for hh in range(2):
            h = 2 * hp + hh
            vs = slice(h * B_VAL_DIM, (h + 1) * B_VAL_DIM)
            vh = v_ref[rows, vs]
            state = s_ref[h]
            lhs = jnp.concatenate([qe[:, hh * B_KEY_DIM:(hh + 1) * B_KEY_DIM], att], axis=1)
            rhs = jnp.concatenate([state.astype(BF16)] + ([vh, zero_v] if hh == 0 else [zero_v, vh]), axis=0)
            o_ref[rows, vs] = jnp.dot(lhs, rhs, preferred_element_type=F32).astype(o_ref.dtype)
            decay = d_ref[h * B_KEY_DIM:(h + 1) * B_KEY_DIM, d_col0 + cc:d_col0 + cc + 1]
            s_ref[h] = state * decay + updates[hh]

    prepared = {}
    width = 2
    n_steps = len(units) // width
    for n in range(n_steps + 1):
        for u in range(width):
            if n < n_steps:
                prepared[n * width + u] = prepare(*units[n * width + u])
        for u in range(width):
            if n >= 1:
                i1 = (n - 1) * width + u
                apply(prepared.pop(i1), *units[i1])


def _gla_scan(v, decayed_f, decayed_b, tot, batch, seq_len):
    n = v.shape[0]
    ts = GLA_TILE
    nt = seq_len // ts
    tiles_in = ts // TOKEN_TILE
    n_tot = tot.shape[1]

    def fwd_tile(b, i):
        return b * nt + i

    def bwd_tile(b, i):
        return b * nt + nt - 1 - i

    def token_specs(tile_of):
        rows = lambda b, i: (tile_of(b, i), 0)
        return [pl.BlockSpec((ts, B_QK_COLS), rows)] * 3 + [
            pl.BlockSpec((ts, B_V_COLS), rows),
            pl.BlockSpec((tiles_in, n_tot, B_QK_COLS), lambda b, i: (tile_of(b, i), 0, 0))]

    return pl.pallas_call(
        _gla_scan_kernel,
        grid=(batch, nt),
        in_specs=token_specs(fwd_tile) + token_specs(bwd_tile),
        out_specs=[pl.BlockSpec((ts, B_V_COLS), lambda b, i: (fwd_tile(b, i), 0)),
                   pl.BlockSpec((ts, B_V_COLS), lambda b, i: (bwd_tile(b, i), 0))],
        out_shape=[jax.ShapeDtypeStruct((n, B_V_COLS), BF16), jax.ShapeDtypeStruct((n, B_V_COLS), BF16)],
        scratch_shapes=[
            pltpu.VMEM((B_QK_COLS, tiles_in * n_tot), F32),
            pltpu.VMEM((B_HEADS, B_KEY_DIM, B_VAL_DIM), F32),
            pltpu.VMEM((B_HEADS, B_KEY_DIM, B_VAL_DIM), F32),
        ],
        compiler_params=_compiler_params(("parallel", "arbitrary")),
        name="gla_scan",
    )(*decayed_f, v, tot, *decayed_b, v, tot)


def _row(v):
    return v.reshape(1, -1).astype(F32)


def _pad_gate_weight(w2, first_row):
    out = jnp.zeros((LANES, w2.shape[1]), BF16)
    return lax.dynamic_update_slice(out, w2.astype(BF16), (first_row, 0))


def kernel(x, positions, attn_w_in, attn_sink, attn_w_out, gla_w_in, gla_gate_w2_fwd, gla_gate_b_fwd,
           gla_gate_w2_bwd, gla_gate_b_bwd, gla_norm_g, gla_w_out, mix_ln_g, mix_ln_b, mlp_w1, mlp_w2,
           mlp_ln_g, mlp_ln_b):
    batch, seq_len, d_model = x.shape
    n = batch * seq_len
    assert d_model == D_MODEL
    assert seq_len % ATTN_Q_TILE == 0 and seq_len % GLA_TILE == 0 and n % ATTN_POST_TILE == 0
    x2 = x.reshape(n, d_model)

    inv_freq = ROPE_THETA ** (-jnp.arange(0, A_HEAD_DIM, 2, dtype=F32) / A_HEAD_DIM)
    ang_t = inv_freq[:, None] * positions.astype(F32).reshape(1, n)
    q_cols = A_HEADS * A_HEAD_DIM
    k_cols = A_KV_HEADS * A_HEAD_DIM

    for i in range(DEPTH):
        j = i // 2
        w1 = mlp_w1[i].astype(BF16)
        w2 = mlp_w2[i].astype(BF16)
        tail = (_row(mix_ln_g[i]), _row(mix_ln_b[i]), w1, w2, _row(mlp_ln_g[i]), _row(mlp_ln_b[i]))
        if i % 2 == 0:
            w_in = attn_w_in[j].astype(BF16)
            wqv_t = jnp.concatenate([w_in[:, :q_cols], w_in[:, q_cols + k_cols:]], axis=1).T
            qt, kd, vt = _attn_in_proj(x2, ang_t, wqv_t, w_in[:, q_cols:q_cols + k_cols])
            o = _window_attention(qt, kd, vt, attn_sink[j].astype(F32), batch, seq_len)
            x2 = _post_mixer(_attn_post_kernel, ATTN_POST_TILE, [o], [], x2, attn_w_out[j].astype(BF16), *tail)
        else:
            w_in = gla_w_in[j]
            lr_cols = 2 * B_GATE_RANK
            main_cols = w_in.shape[1] - lr_cols
            w_in = jnp.concatenate([jnp.pad(w_in[:, main_cols:], ((0, 0), (0, LANES - lr_cols))),
                                    w_in[:, :main_cols]], axis=1).astype(BF16)
            v, r, *decayed, tot = _gla_in_proj(
                x2, w_in, _pad_gate_weight(gla_gate_w2_fwd[j], 0), _row(gla_gate_b_fwd[j]),
                _pad_gate_weight(gla_gate_w2_bwd[j], B_GATE_RANK), _row(gla_gate_b_bwd[j]))
            o_f, o_b = _gla_scan(v, decayed[:3], decayed[3:], tot, batch, seq_len)
            x2 = _post_mixer(_gla_post_kernel, TOKEN_TILE, [o_f, o_b, r], [_row(gla_norm_g[j])], x2,
                             gla_w_out[j].astype(BF16), *tail)
    return x2.reshape(batch, seq_len, d_model)
```
